```python
import math
import jax, jax.numpy as jnp
from jax import lax
import numpy as np

D_MODEL = 2048
BATCH = 16
SEQ = 2048
DEPTH = 2

CTX_LEN = 256
GRID_W = 64

ALPHA = (2 * DEPTH) ** 0.25
BETA = (8 * DEPTH) ** -0.25
LN_EPS = 1e-5

D_MIX = D_MODEL
SSD_WIDTH = D_MIX // 2
SSD_HEADDIM = 64
SSD_HEADS = SSD_WIDTH // SSD_HEADDIM
SSD_GROUPS = 2
SSD_HPG = SSD_HEADS // SSD_GROUPS
SSD_STATE = 128
SSD_CONV = 3
SSD_CHUNK = 128
SSD_CONV_DIM = SSD_WIDTH + 2 * SSD_GROUPS * SSD_STATE
FNET_WIDTH = D_MIX - SSD_WIDTH
FNET_GROUPS = 4
FNET_GW = FNET_WIDTH // FNET_GROUPS
IN_PROJ_A = SSD_WIDTH + SSD_CONV_DIM + 2 * SSD_HEADS + FNET_WIDTH

RWKV_HEADSIZE = 64
RWKV_HEADS = D_MODEL // RWKV_HEADSIZE
DECAY_LORA = 96
AAA_LORA = 96
GATE_LORA = 256
GN_EPS = 64e-5

N_EXPERTS = 32
N_GROUPS = 4
EXPERTS_PER_GROUP = N_EXPERTS // N_GROUPS
TOP_K = 2
D_EXPERT = 512
MOE_BLOCK = 128

N_EVEN = (DEPTH + 1) // 2
N_ODD = DEPTH // 2

kernel_name = "ssd_fnet_rwkv7_grouped_moe_deepnorm_prefix"


def layer_norm(x, g, b):
    xf = x.astype(jnp.float32)
    mu = jnp.mean(xf, -1, keepdims=True)
    var = jnp.mean(jnp.square(xf - mu), -1, keepdims=True)
    return ((xf - mu) * lax.rsqrt(var + LN_EPS) * g + b).astype(x.dtype)


def dwconv_centred(x, w, b):
    K, C = w.shape
    y = lax.conv_general_dilated(x, w[:, None, :].astype(x.dtype), window_strides=(1,),
                                 padding=[(K // 2, K // 2)],
                                 dimension_numbers=("NWC", "WIO", "NWC"),
                                 feature_group_count=C)
    return y + b


def ssd_chunked(xh, dt, dA, Bm, Cm, s0):
    b, L, g, r, p = xh.shape
    n = Bm.shape[-1]
    Q = SSD_CHUNK
    nc = L // Q
    x = xh.reshape(b, nc, Q, g, r, p)
    dtc = dt.reshape(b, nc, Q, g, r)
    Bc = Bm.reshape(b, nc, Q, g, n)
    Cc = Cm.reshape(b, nc, Q, g, n)
    a_cum = jnp.cumsum(dA.astype(jnp.float32).reshape(b, nc, Q, g, r), axis=2)
    a_t = jnp.moveaxis(a_cum, 2, -1)
    seg = a_t[..., :, None] - a_t[..., None, :]
    causal = jnp.tril(jnp.ones((Q, Q), bool))
    decay = jnp.exp(jnp.where(causal, seg, -jnp.inf))
    scores = jnp.einsum("bclgn,bcsgn->bcgls", Cc, Bc)
    xdt = x * dtc[..., None]
    y_diag = jnp.einsum("bcgrls,bcsgrp->bclgrp", scores[:, :, :, None] * decay, xdt)
    decay_to_end = jnp.exp(a_cum[:, :, -1:] - a_cum)
    states = jnp.einsum("bcsgn,bcsgrp->bcgrpn", Bc, xdt * decay_to_end[..., None])
    chunk_decay = jnp.exp(a_cum[:, :, -1])

    def step(s, inp):
        st, cd = inp
        return s * cd[..., None, None] + st, s

    s_fin, s_in = lax.scan(step, s0, (jnp.moveaxis(states, 1, 0), jnp.moveaxis(chunk_decay, 1, 0)))
    s_in = jnp.moveaxis(s_in, 0, 1)
    y_off = jnp.einsum("bclgn,bcgrpn->bclgrp", Cc, s_in) * jnp.exp(a_cum)[..., None]
    return (y_diag + y_off).reshape(b, L, g, r, p), s_fin


def fourier_mix(f):
    b, L, _ = f.shape
    fg = f.astype(jnp.float32).reshape(b, L, FNET_GROUPS, FNET_GW)
    out = jnp.fft.fft2(fg, axes=(1, 3), norm="ortho").real
    return out.reshape(b, L, FNET_WIDTH).astype(f.dtype)


def even_mixer(h_lat, h_ctx, w_in, conv_w, conv_b, a_log, dt_bias, d_skip, norm_w, w_out, need_ctx):
    A = -jnp.exp(a_log.astype(jnp.float32))

    def split_inputs(h):
        b, L, _ = h.shape
        pr = h @ w_in
        o1 = SSD_WIDTH
        o2 = o1 + SSD_CONV_DIM
        o3 = o2 + 2 * SSD_HEADS
        z, xbc, dtr, f = pr[..., :o1], pr[..., o1:o2], pr[..., o2:o3], pr[..., o3:]
        xbc = jax.nn.silu(dwconv_centred(xbc, conv_w, conv_b))
        gn = SSD_GROUPS * SSD_STATE
        xs = xbc[..., :SSD_WIDTH].reshape(b, L, SSD_GROUPS, SSD_HPG, SSD_HEADDIM)
        Bm = xbc[..., SSD_WIDTH:SSD_WIDTH + gn].reshape(b, L, SSD_GROUPS, SSD_STATE)
        Cm = xbc[..., SSD_WIDTH + gn:].reshape(b, L, SSD_GROUPS, SSD_STATE)
        dt = jax.nn.softplus(dtr.astype(jnp.float32).reshape(b, L, 2, SSD_HEADS) + dt_bias)
        return z, xs, Bm, Cm, dt, f

    def scan_both(xs, Bm, Cm, dt, s0):
        b, L = xs.shape[:2]
        flip = lambda t: jnp.flip(t, axis=1)
        x2 = jnp.concatenate([xs, flip(xs)], 0)
        B2 = jnp.concatenate([Bm, flip(Bm)], 0)
        C2 = jnp.concatenate([Cm, flip(Cm)], 0)
        dt2 = jnp.concatenate([dt[:, :, 0], flip(dt[:, :, 1])], 0)
        dA2 = dt2 * jnp.repeat(A, b, axis=0)[:, None, :]
        shp = (2 * b, L, SSD_GROUPS, SSD_HPG)
        y2, s_fin = ssd_chunked(x2, dt2.reshape(shp), dA2.reshape(shp), B2, C2, s0)
        return y2[:b] + flip(y2[b:]), s_fin

    def ssd_out(y, xs, z):
        b, L = y.shape[:2]
        y = y + d_skip.reshape(SSD_GROUPS, SSD_HPG)[..., None] * xs
        u = (y.reshape(b, L, SSD_WIDTH) * jax.nn.silu(z.astype(jnp.float32))).reshape(b, L, SSD_GROUPS, -1)
        u = u * lax.rsqrt(jnp.mean(jnp.square(u), -1, keepdims=True) + LN_EPS)
        return (u.reshape(b, L, SSD_WIDTH) * norm_w).astype(z.dtype)

    bsz = h_ctx.shape[0]
    s0 = jnp.zeros((2 * bsz, SSD_GROUPS, SSD_HPG, SSD_HEADDIM, SSD_STATE), jnp.float32)
    zc, xc, Bc, Cc, dtc, fc = split_inputs(h_ctx)
    yc, s_ctx = scan_both(xc, Bc, Cc, dtc, s0)
    zl, xl, Bl, Cl, dtl, fl = split_inputs(h_lat)
    yl, _ = scan_both(xl, Bl, Cl, dtl, s_ctx)
    y_lat = jnp.concatenate([ssd_out(yl, xl, zl), fourier_mix(fl)], -1) @ w_out
    y_ctx = None
    if need_ctx:
        y_ctx = jnp.concatenate([ssd_out(yc, xc, zc), fourier_mix(fc)], -1) @ w_out
    return y_lat, y_ctx


def grid_shift(x):
    b, L, D = x.shape
    rows = L // GRID_W
    q = D // 4
    xg = x.reshape(b, rows, GRID_W, D)
    left = jnp.pad(xg[:, :, :-1, :q], ((0, 0), (0, 0), (1, 0), (0, 0)))
    right = jnp.pad(xg[:, :, 1:, q:2 * q], ((0, 0), (0, 0), (0, 1), (0, 0)))
    up = jnp.pad(xg[:, :-1, :, 2 * q:3 * q], ((0, 0), (1, 0), (0, 0), (0, 0)))
    down = jnp.pad(xg[:, 1:, :, 3 * q:], ((0, 0), (0, 1), (0, 0), (0, 0)))
    return jnp.concatenate([left, right, up, down], -1).reshape(b, L, D)


def seq_shift(x):
    D = x.shape[-1]
    q = D // 4
    prev = jnp.pad(x[:, :-1], ((0, 0), (1, 0), (0, 0)))
    nxt = jnp.pad(x[:, 1:], ((0, 0), (0, 1), (0, 0)))
    return jnp.concatenate([prev[..., :q], nxt[..., q:2 * q], prev[..., 2 * q:3 * q], nxt[..., 3 * q:]], -1)


def rwkv_inputs(h, shifted, mu, w_rkv, w0, w1, w2, a0, a1, a2, g1, g2, k_k, k_a):
    f32 = jnp.float32
    heads = lambda t: t.reshape(t.shape[:-1] + (RWKV_HEADS, RWKV_HEADSIZE))
    xx = shifted - h
    xr, xw, xk, xv, xa, xg = (h + xx * mu[i] for i in range(6))
    r = xr @ w_rkv[0]
    k = xk @ w_rkv[1]
    v = xv @ w_rkv[2]
    dw = jnp.einsum("eblr,erd->ebld", jnp.tanh(jnp.einsum("bld,edr->eblr", xw, w1)), w2)
    w_log = -jax.nn.softplus(-(w0[:, None, None, :] + dw).astype(f32)) - 0.5
    decay = jnp.exp(-jnp.exp(w_log))
    a = jax.nn.sigmoid((a0[:, None, None, :]
                        + jnp.einsum("eblr,erd->ebld", jnp.einsum("bld,edr->eblr", xa, a1), a2)).astype(f32))
    g = jax.nn.sigmoid(xg @ g1) @ g2
    kk = heads((k * k_k).astype(f32))
    kk = kk / jnp.maximum(jnp.sqrt(jnp.sum(jnp.square(kk), -1, keepdims=True)), 1e-12)
    k_dir = k.astype(f32)[None] * (1.0 + (a - 1.0) * k_a)
    return (heads(r.astype(f32)), kk, heads(v.astype(f32)), heads(decay), heads(k_dir), heads(a),
            heads(k.astype(f32)), g)


def wkv_bidir(r, kk, v, w, k, a, s0):
    L = r.shape[1]

    def take(arr, t):
        return lax.dynamic_index_in_dim(arr, t, axis=1, keepdims=False)

    def step(S, t):
        tb = L - 1 - t
        r_t = jnp.stack([take(r, t), take(r, tb)])
        kk_t = jnp.stack([take(kk, t), take(kk, tb)])
        v_t = jnp.stack([take(v, t), take(v, tb)])
        w_t = jnp.stack([take(w[0], t), take(w[1], tb)])
        k_t = jnp.stack([take(k[0], t), take(k[1], tb)])
        a_t = jnp.stack([take(a[0], t), take(a[1], tb)])
        sa = jnp.einsum("ebhvk,ebhk->ebhv", S, kk_t)
        S = (S * w_t[..., None, :] - sa[..., None] * (kk_t * a_t)[..., None, :]
             + v_t[..., :, None] * k_t[..., None, :])
        return S, jnp.einsum("ebhvk,ebhk->ebhv", S, r_t)

    S_fin, ys = lax.scan(step, s0, jnp.arange(L))
    y = ys[:, 0] + ys[::-1, 1]
    return jnp.moveaxis(y, 0, 1), S_fin


def rwkv_readout(y, r, k, v, g, r_k, lnx_w, lnx_b, w_o, dtype):
    b, L = y.shape[:2]
    mu = jnp.mean(y, -1, keepdims=True)
    var = jnp.mean(jnp.square(y - mu), -1, keepdims=True)
    yn = ((y - mu) * lax.rsqrt(var + GN_EPS)).reshape(b, L, D_MODEL) * lnx_w + lnx_b
    bonus = (jnp.sum(r * k * r_k, -1, keepdims=True) * v).reshape(b, L, D_MODEL)
    return ((yn + bonus) * g).astype(dtype) @ w_o


def odd_mixer(h_lat, h_ctx, mu, w_rkv, w_o, w0, w1, w2, a0, a1, a2, g1, g2, k_k, k_a, r_k,
              lnx_w, lnx_b, need_ctx):
    def inputs(h, shifted):
        return rwkv_inputs(h, shifted, mu, w_rkv, w0, w1, w2, a0, a1, a2, g1, g2, k_k, k_a)

    bsz = h_ctx.shape[0]
    s0 = jnp.zeros((2, bsz, RWKV_HEADS, RWKV_HEADSIZE, RWKV_HEADSIZE), jnp.float32)
    rc, kkc, vc, wc, kdc, ac, kc, gc = inputs(h_ctx, seq_shift(h_ctx))
    yc, s_ctx = wkv_bidir(rc, kkc, vc, wc, kdc, ac, s0)
    rl, kkl, vl, wl, kdl, al, kl, gl = inputs(h_lat, grid_shift(h_lat))
    yl, _ = wkv_bidir(rl, kkl, vl, wl, kdl, al, s_ctx)
    y_lat = rwkv_readout(yl, rl, kl, vl, gl, r_k, lnx_w, lnx_b, w_o, h_lat.dtype)
    y_ctx = None
    if need_ctx:
        y_ctx = rwkv_readout(yc, rc, kc, vc, gc, r_k, lnx_w, lnx_b, w_o, h_ctx.dtype)
    return y_lat, y_ctx


def moe(h, router_w, router_bias, w_gate, w_up, w_down):
    T, D = h.shape
    s = jax.nn.sigmoid((h @ router_w).astype(jnp.float32))
    s_sel = s + router_bias
    group_score = lax.top_k(s_sel.reshape(T, N_GROUPS, EXPERTS_PER_GROUP), 2)[0].sum(-1)
    gsel = jnp.argmax(group_score, -1)
    in_group = (jnp.arange(N_EXPERTS) // EXPERTS_PER_GROUP)[None, :] == gsel[:, None]
    _, idx = lax.top_k(jnp.where(in_group, s_sel, -jnp.inf), TOP_K)
    wts = jnp.take_along_axis(s, idx, -1)
    wts = wts / jnp.sum(wts, -1, keepdims=True)

    A = T * TOP_K
    e_flat = idx.reshape(A)
    tok = jnp.repeat(jnp.arange(T), TOP_K)
    order = jnp.argsort(e_flat)
    e_s, tok_s, g_s = e_flat[order], tok[order], wts.reshape(A)[order]
    counts = jnp.bincount(e_flat, length=N_EXPERTS)
    padded = (counts + MOE_BLOCK - 1) // MOE_BLOCK * MOE_BLOCK
    start = jnp.cumsum(counts) - counts
    ends = jnp.cumsum(padded)
    pstart = ends - padded
    dest = pstart[e_s] + jnp.arange(A) - start[e_s]
    n_blocks = -(-(A + N_EXPERTS * (MOE_BLOCK - 1)) // MOE_BLOCK)
    P = n_blocks * MOE_BLOCK
    xbuf = jnp.zeros((P, D), h.dtype).at[dest].set(h[tok_s])
    block_e = jnp.minimum(jnp.searchsorted(ends, jnp.arange(n_blocks) * MOE_BLOCK, side="right"),
                          N_EXPERTS - 1)

    def expert_block(args):
        xb, e = args
        hid = jax.nn.silu(xb @ w_gate[e]) * (xb @ w_up[e])
        return hid @ w_down[e]

    ybuf = lax.map(expert_block, (xbuf.reshape(n_blocks, MOE_BLOCK, D), block_e)).reshape(P, D)
    y_s = ybuf[dest] * g_s[:, None].astype(h.dtype)
    return jnp.zeros((T, D), h.dtype).at[tok_s].add(y_s)


def setup_inputs(seed: int = 0) -> dict:
    key = jax.random.key(seed)
    ks = iter(jax.random.split(key, 64))
    f32 = jnp.float32
    D = D_MODEL

    def nrm(shape, scale):
        return jax.random.normal(next(ks), shape, f32) * scale

    def near_one(shape):
        return 1.0 + nrm(shape, 0.02)

    def unif(shape, lo, hi):
        return jax.random.uniform(next(ks), shape, f32, lo, hi)

    x = nrm((BATCH, SEQ, D), 1.0)
    c = nrm((BATCH, D), 1.0)
    ctx = nrm((BATCH, CTX_LEN, D), 1.0)
    c_ctx = nrm((D,), 1.0)
    w_mod = nrm((DEPTH, D, 6 * D), D ** -0.5)
    b_mod = nrm((DEPTH, 6 * D), 0.02)
    ln_g = near_one((DEPTH, 2, D))
    ln_b = nrm((DEPTH, 2, D), 0.02)

    ssd_w_in = nrm((N_EVEN, D, IN_PROJ_A), D ** -0.5)
    ssd_conv_w = nrm((N_EVEN, SSD_CONV, SSD_CONV_DIM), SSD_CONV ** -0.5)
    ssd_conv_b = nrm((N_EVEN, SSD_CONV_DIM), 0.02)
    ssd_a_log = jnp.log(unif((N_EVEN, 2, SSD_HEADS), 1.0, 16.0))
    dt0 = jnp.exp(unif((N_EVEN, 2, SSD_HEADS), math.log(1e-3), math.log(1e-1)))
    ssd_dt_bias = dt0 + jnp.log(-jnp.expm1(-dt0))
    ssd_d = near_one((N_EVEN, SSD_HEADS))
    ssd_norm_w = near_one((N_EVEN, SSD_WIDTH))
    even_w_out = nrm((N_EVEN, D_MIX, D), D_MIX ** -0.5 * BETA)

    rwkv_mu = unif((N_ODD, 6, D), 0.0, 1.0)
    rwkv_w_rkv = nrm((N_ODD, 3, D, D), D ** -0.5)
    rwkv_w_o = nrm((N_ODD, D, D), D ** -0.5 * BETA)
    rwkv_w0 = unif((N_ODD, 2, D), -6.0, -1.0)
    rwkv_w1 = nrm((N_ODD, 2, D, DECAY_LORA), D ** -0.5)
    rwkv_w2 = nrm((N_ODD, 2, DECAY_LORA, D), 0.1 * DECAY_LORA ** -0.5)
    rwkv_a0 = nrm((N_ODD, 2, D), 0.1)
    rwkv_a1 = nrm((N_ODD, 2, D, AAA_LORA), D ** -0.5)
    rwkv_a2 = nrm((N_ODD, 2, AAA_LORA, D), 0.1 * AAA_LORA ** -0.5)
    rwkv_g1 = nrm((N_ODD, D, GATE_LORA), D ** -0.5)
    rwkv_g2 = nrm((N_ODD, GATE_LORA, D), GATE_LORA ** -0.5)
    rwkv_k_k = 0.85 + nrm((N_ODD, D), 0.02)
    rwkv_k_a = near_one((N_ODD, D))
    rwkv_r_k = nrm((N_ODD, RWKV_HEADS, RWKV_HEADSIZE), 0.1)
    rwkv_lnx_w = near_one((N_ODD, D))
    rwkv_lnx_b = nrm((N_ODD, D), 0.02)

    router_w = nrm((D, N_EXPERTS), D ** -0.5)
    router_bias = nrm((N_EXPERTS,), 0.01)
    moe_w_gate = nrm((DEPTH, N_EXPERTS, D, D_EXPERT), D ** -0.5)
    moe_w_up = nrm((DEPTH, N_EXPERTS, D, D_EXPERT), D ** -0.5)
    moe_w_down = nrm((DEPTH, N_EXPERTS, D_EXPERT, D), D_EXPERT ** -0.5 * BETA)
    return {"x": x, "c": c, "ctx": ctx, "c_ctx": c_ctx, "w_mod": w_mod, "b_mod": b_mod,
            "ln_g": ln_g, "ln_b": ln_b, "ssd_w_in": ssd_w_in, "ssd_conv_w": ssd_conv_w,
            "ssd_conv_b": ssd_conv_b, "ssd_a_log": ssd_a_log, "ssd_dt_bias": ssd_dt_bias, "ssd_d": ssd_d,
            "ssd_norm_w": ssd_norm_w, "even_w_out": even_w_out, "rwkv_mu": rwkv_mu,
            "rwkv_w_rkv": rwkv_w_rkv, "rwkv_w_o": rwkv_w_o, "rwkv_w0": rwkv_w0, "rwkv_w1": rwkv_w1,
            "rwkv_w2": rwkv_w2, "rwkv_a0": rwkv_a0, "rwkv_a1": rwkv_a1, "rwkv_a2": rwkv_a2,
            "rwkv_g1": rwkv_g1, "rwkv_g2": rwkv_g2, "rwkv_k_k": rwkv_k_k, "rwkv_k_a": rwkv_k_a,
            "rwkv_r_k": rwkv_r_k, "rwkv_lnx_w": rwkv_lnx_w, "rwkv_lnx_b": rwkv_lnx_b,
            "router_w": router_w, "router_bias": router_bias, "moe_w_gate": moe_w_gate,
            "moe_w_up": moe_w_up, "moe_w_down": moe_w_down}


def reference(x, c, ctx, c_ctx, w_mod, b_mod, ln_g, ln_b, ssd_w_in, ssd_conv_w, ssd_conv_b, ssd_a_log,
              ssd_dt_bias, ssd_d, ssd_norm_w, even_w_out, rwkv_mu, rwkv_w_rkv, rwkv_w_o, rwkv_w0, rwkv_w1,
              rwkv_w2, rwkv_a0, rwkv_a1, rwkv_a2, rwkv_g1, rwkv_g2, rwkv_k_k, rwkv_k_a, rwkv_r_k,
              rwkv_lnx_w, rwkv_lnx_b, router_w, router_bias, moe_w_gate, moe_w_up, moe_w_down):
    x_lat, x_ctx = x, ctx
    for i in range(DEPTH):
        last = i == DEPTH - 1
        j = i // 2
        ml = [m[:, None, :] for m in jnp.split(jax.nn.silu(c) @ w_mod[i] + b_mod[i], 6, axis=-1)]
        mc = jnp.split(jax.nn.silu(c_ctx) @ w_mod[i] + b_mod[i], 6, axis=-1)
        h_lat = x_lat * (1 + ml[1]) + ml[0]
        h_ctx = x_ctx * (1 + mc[1]) + mc[0]
        if i % 2 == 0:
            y_lat, y_ctx = even_mixer(h_lat, h_ctx, ssd_w_in[j], ssd_conv_w[j], ssd_conv_b[j], ssd_a_log[j],
                                      ssd_dt_bias[j], ssd_d[j], ssd_norm_w[j], even_w_out[j], not last)
        else:
            y_lat, y_ctx = odd_mixer(h_lat, h_ctx, rwkv_mu[j], rwkv_w_rkv[j], rwkv_w_o[j], rwkv_w0[j],
                                     rwkv_w1[j], rwkv_w2[j], rwkv_a0[j], rwkv_a1[j], rwkv_a2[j], rwkv_g1[j],
                                     rwkv_g2[j], rwkv_k_k[j], rwkv_k_a[j], rwkv_r_k[j], rwkv_lnx_w[j],
                                     rwkv_lnx_b[j], not last)
        x_lat = layer_norm(ALPHA * x_lat + ml[2] * y_lat, ln_g[i, 0], ln_b[i, 0])
        h2_lat = x_lat * (1 + ml[4]) + ml[3]
        if last:
            y2_lat = moe(h2_lat.reshape(-1, D_MODEL), router_w, router_bias, moe_w_gate[i], moe_w_up[i],
                         moe_w_down[i]).reshape(x_lat.shape)
        else:
            x_ctx = layer_norm(ALPHA * x_ctx + mc[2] * y_ctx, ln_g[i, 0], ln_b[i, 0])
            h2_ctx = x_ctx * (1 + mc[4]) + mc[3]
            n_lat = h2_lat.shape[0] * h2_lat.shape[1]
            tokens = jnp.concatenate([h2_lat.reshape(-1, D_MODEL), h2_ctx.reshape(-1, D_MODEL)], 0)
            y2 = moe(tokens, router_w, router_bias, moe_w_gate[i], moe_w_up[i], moe_w_down[i])
            y2_lat = y2[:n_lat].reshape(x_lat.shape)
            x_ctx = layer_norm(ALPHA * x_ctx + mc[5] * y2[n_lat:].reshape(x_ctx.shape), ln_g[i, 1], ln_b[i, 1])
        x_lat = layer_norm(ALPHA * x_lat + ml[5] * y2_lat, ln_g[i, 1], ln_b[i, 1])
    return x_lat
```

```python
import functools
import math

import numpy as np
import jax
import jax.numpy as jnp
from jax import lax
from jax.experimental import pallas as pl
from jax.experimental.pallas import tpu as pltpu

F32 = jnp.float32
BF16 = jnp.bfloat16

V7X_VMEM_BYTES = 64 * 1024 * 1024
VMEM_LIMIT = V7X_VMEM_BYTES - 8 * 1024 * 1024
LANES = 128

D_MODEL = 2048
DEPTH = 2
ALPHA = (2 * DEPTH) ** 0.25
LN_EPS = 1e-5
GN_EPS = 64e-5

SSD_WIDTH = 1024
SSD_HEADDIM = 64
SSD_HEADS = 16
SSD_GROUPS = 2
SSD_HPG = 8
SSD_STATE = 128
SSD_CONV_DIM = SSD_WIDTH + 2 * SSD_GROUPS * SSD_STATE
FNET_WIDTH = 1024
FNET_GROUPS = 4
FNET_GW = 256
SSD_Q = 128
PR_Z, PR_XBC, PR_F, PR_DT, PR_N = 0, 1024, 2560, 3584, 3840

RWKV_HEADSIZE = 64
RWKV_HEADS = 32
RWKV_PAIRS = 16
DECAY_LORA = 96
AAA_LORA = 96
GATE_LORA = 256
LORA_PAD = 256
RW_N = 3 * D_MODEL + 3 * LORA_PAD
WKV_T = 64

N_EXPERTS = 32
N_GROUPS = 4
EXPERTS_PER_GROUP = 8
TOP_K = 2
D_EXPERT = 512
MOE_BM = 256


def _cparams(sem):
    return pltpu.CompilerParams(dimension_semantics=sem, vmem_limit_bytes=VMEM_LIMIT)


def _dot(a, b):
    return jnp.dot(a, b, preferred_element_type=F32)


def _dot_nt(a, b):
    return lax.dot_general(a, b, (((1,), (1,)), ((), ())), preferred_element_type=F32)


def _dot_tn(a, b):
    return lax.dot_general(a, b, (((0,), (0,)), ((), ())), preferred_element_type=F32)


def _split2(a):
    hi = a.astype(BF16)
    lo = (a - hi.astype(F32)).astype(BF16)
    return hi, lo


def _split3(a):
    hi = a.astype(BF16)
    r1 = a - hi.astype(F32)
    mid = r1.astype(BF16)
    lo = (r1 - mid.astype(F32)).astype(BF16)
    return hi, mid, lo


def _dot_exact_lhs(a_bf16, b_f32):
    b1, b2, b3 = _split3(b_f32)
    return _dot(a_bf16, b1) + _dot(a_bf16, b2) + _dot(a_bf16, b3)


def _dot_exact_rhs(a_f32, b_bf16):
    a1, a2, a3 = _split3(a_f32)
    return _dot(a1, b_bf16) + _dot(a2, b_bf16) + _dot(a3, b_bf16)


def _dot_x3(a_f32, b_f32):
    a1, a2 = _split2(a_f32)
    b1, b2 = _split2(b_f32)
    return _dot(a1, b1) + _dot(a1, b2) + _dot(a2, b1)


def _silu(x):
    return x * jax.nn.sigmoid(x)


def _softplus(x):
    return jnp.maximum(x, 0.0) + jnp.log(1.0 + jnp.exp(-jnp.abs(x)))


def _layer_norm(v, g, b):
    mu = jnp.mean(v, axis=-1, keepdims=True)
    d = v - mu
    var = jnp.mean(d * d, axis=-1, keepdims=True)
    return d * lax.rsqrt(var + LN_EPS) * g + b


def _dense_silu_kernel(x_ref, w_ref, b_ref, o_ref):
    x = _silu(x_ref[...]).astype(BF16)
    o_ref[...] = _dot(x, w_ref[...].astype(BF16)) + b_ref[...]


def dense_silu(x, w, bias, tn=512):
    m, k = x.shape
    n = w.shape[1]
    return pl.pallas_call(
        _dense_silu_kernel,
        grid=(n // tn,),
        in_specs=[pl.BlockSpec((m, k), lambda j: (0, 0)),
                  pl.BlockSpec((k, tn), lambda j: (0, j)),
                  pl.BlockSpec((1, tn), lambda j: (0, j))],
        out_specs=pl.BlockSpec((m, tn), lambda j: (0, j)),
        out_shape=jax.ShapeDtypeStruct((m, n), F32),
        compiler_params=_cparams(("parallel",)),
        name="dense_silu",
    )(x, w, bias)


def _modmm_kernel(x_ref, sc_ref, sh_ref, w_ref, o_ref, h_ref):
    @pl.when(pl.program_id(2) == 0)
    def _():
        h_ref[...] = (x_ref[...] * (1.0 + sc_ref[...]) + sh_ref[...]).astype(BF16)

    o_ref[...] = _dot(h_ref[...], w_ref[...]).astype(o_ref.dtype)


def modmm(x, sc, sh, w, tm, tn, out_dtype=F32):
    b, L, k = x.shape
    n = w.shape[1]
    return pl.pallas_call(
        _modmm_kernel,
        grid=(b, L // tm, n // tn),
        in_specs=[pl.BlockSpec((None, tm, k), lambda bi, i, j: (bi, i, 0)),
                  pl.BlockSpec((None, 1, k), lambda bi, i, j: (bi, 0, 0)),
                  pl.BlockSpec((None, 1, k), lambda bi, i, j: (bi, 0, 0)),
                  pl.BlockSpec((k, tn), lambda bi, i, j: (0, j))],
        out_specs=pl.BlockSpec((None, tm, tn), lambda bi, i, j: (bi, i, j)),
        out_shape=jax.ShapeDtypeStruct((b, L, n), out_dtype),
        scratch_shapes=[pltpu.VMEM((tm, k), BF16)],
        compiler_params=_cparams(("parallel", "parallel", "arbitrary")),
        name="modmm",
    )(x, sc, sh, w)


def _conv_silu_kernel(x_ref, w_ref, b_ref, o_ref):
    x = x_ref[...]
    L = x.shape[0]
    row = lax.broadcasted_iota(jnp.int32, x.shape, 0)
    prev = jnp.where(row == 0, 0.0, pltpu.roll(x, 1, axis=0))
    nxt = jnp.where(row == L - 1, 0.0, pltpu.roll(x, L - 1, axis=0))
    w = w_ref[...]
    y = prev * w[0:1] + x * w[1:2] + nxt * w[2:3] + b_ref[...]
    o_ref[...] = _silu(y)


def conv_silu(pr, conv_w, conv_b, tc=512):
    b, L, _ = pr.shape
    off = PR_XBC // tc
    return pl.pallas_call(
        _conv_silu_kernel,
        grid=(b, SSD_CONV_DIM // tc),
        in_specs=[pl.BlockSpec((None, L, tc), lambda bi, j: (bi, 0, off + j)),
                  pl.BlockSpec((3, tc), lambda bi, j: (0, j)),
                  pl.BlockSpec((1, tc), lambda bi, j: (0, j))],
        out_specs=pl.BlockSpec((None, L, tc), lambda bi, j: (bi, 0, j)),
        out_shape=jax.ShapeDtypeStruct((b, L, SSD_CONV_DIM), F32),
        compiler_params=_cparams(("parallel", "parallel")),
        name="conv_silu",
    )(pr, conv_w, conv_b)


def _ssd_kernel(x_ref, bc_ref, dtr_ref, dtb_ref, alog_ref, s0_ref, y_ref, sfin_ref, st_ref):
    Q = SSD_Q
    d = pl.program_id(0)
    c = pl.program_id(2)
    nc = pl.num_programs(2)
    fwd = d == 0

    @pl.when(c == 0)
    def _():
        st_ref[...] = s0_ref[...]

    row = lax.broadcasted_iota(jnp.int32, (Q, Q), 0)
    col = lax.broadcasted_iota(jnp.int32, (Q, Q), 1)
    sgn = 1 - 2 * d
    incl = (row - col) * sgn >= 0
    tri = jnp.where(incl, 1.0, 0.0).astype(BF16)

    dt_all = _softplus(dtr_ref[:, 0:LANES] + dtb_ref[...])
    dA_all = dt_all * (-jnp.exp(alog_ref[...]))
    acum_all = _dot_exact_lhs(tri, dA_all)
    acum_t = acum_all.T

    lane_h = lax.broadcasted_iota(jnp.int32, (LANES, SSD_HPG * SSD_HEADDIM), 0)
    col_h = lax.broadcasted_iota(jnp.int32, (LANES, SSD_HPG * SSD_HEADDIM), 1) // SSD_HEADDIM
    lane128 = lax.broadcasted_iota(jnp.int32, (Q, LANES), 1)
    m_lo = lane128 < SSD_HEADDIM

    for g in range(SSD_GROUPS):
        expand = jnp.where(lane_h == d * SSD_HEADS + g * SSD_HPG + col_h, 1.0, 0.0).astype(BF16)
        dt_e = _dot_exact_rhs(dt_all, expand)
        a_e = _dot_exact_rhs(acum_all, expand)
        a_tot = jnp.where(fwd, a_e[Q - 1:Q], a_e[0:1])
        xg = x_ref[:, g * 512:(g + 1) * 512]
        xdt = xg * dt_e
        bg = bc_ref[:, g * SSD_STATE:(g + 1) * SSD_STATE]
        cg = bc_ref[:, SSD_GROUPS * SSD_STATE + g * SSD_STATE:SSD_GROUPS * SSD_STATE + (g + 1) * SSD_STATE]
        bgb = bg.astype(BF16)
        cgb = cg.astype(BF16)
        scores = _dot_nt(cgb, bgb)
        st = st_ref[g]
        y_off = _dot(cgb, st.astype(BF16)) * jnp.exp(a_e)
        xdt_b = xdt.astype(BF16)
        for pr_i in range(SSD_HPG // 2):
            xp = xdt_b[:, pr_i * LANES:(pr_i + 1) * LANES]
            acc = y_off[:, pr_i * LANES:(pr_i + 1) * LANES]
            for half in range(2):
                r = 2 * pr_i + half
                hrow = g * SSD_HPG + r
                a_col = jnp.where(fwd, acum_all[:, hrow:hrow + 1],
                                  acum_all[:, SSD_HEADS + hrow:SSD_HEADS + hrow + 1])
                a_row = jnp.where(fwd, acum_t[hrow:hrow + 1, :],
                                  acum_t[SSD_HEADS + hrow:SSD_HEADS + hrow + 1, :])
                seg = jnp.minimum(a_col - a_row, 0.0)
                m = jnp.where(incl, scores * jnp.exp(seg), 0.0).astype(BF16)
                xh = jnp.where(m_lo if half == 0 else jnp.logical_not(m_lo), xp, jnp.zeros_like(xp))
                acc = acc + _dot(m, xh)
            y_ref[:, g * 512 + pr_i * LANES:g * 512 + (pr_i + 1) * LANES] = acc
        xde = (xdt * jnp.exp(a_tot - a_e)).astype(BF16)
        st_ref[g] = st * jnp.exp(a_tot) + _dot_tn(bgb, xde)

    @pl.when(c == nc - 1)
    def _():
        sfin_ref[...] = st_ref[...]


def ssd_scan(xbc, pr, dt_bias128, a_log128, s0):
    b, L, _ = xbc.shape
    Q = SSD_Q
    nc = L // Q

    def cidx(d, c):
        return jnp.where(d == 0, c, nc - 1 - c)

    return pl.pallas_call(
        _ssd_kernel,
        grid=(2, b, nc),
        in_specs=[pl.BlockSpec((None, Q, SSD_WIDTH), lambda d, bi, c: (bi, cidx(d, c), 0)),
                  pl.BlockSpec((None, Q, 512), lambda d, bi, c: (bi, cidx(d, c), SSD_WIDTH // 512)),
                  pl.BlockSpec((None, Q, 256), lambda d, bi, c: (bi, cidx(d, c), PR_DT // 256)),
                  pl.BlockSpec((1, LANES), lambda d, bi, c: (0, 0)),
                  pl.BlockSpec((1, LANES), lambda d, bi, c: (0, 0)),
                  pl.BlockSpec((None, None, SSD_GROUPS, SSD_STATE, 512), lambda d, bi, c: (d, bi, 0, 0, 0))],
        out_specs=[pl.BlockSpec((None, None, Q, SSD_WIDTH), lambda d, bi, c: (d, bi, cidx(d, c), 0)),
                   pl.BlockSpec((None, None, SSD_GROUPS, SSD_STATE, 512), lambda d, bi, c: (d, bi, 0, 0, 0))],
        out_shape=[jax.ShapeDtypeStruct((2, b, L, SSD_WIDTH), F32),
                   jax.ShapeDtypeStruct((2, b, SSD_GROUPS, SSD_STATE, 512), F32)],
        scratch_shapes=[pltpu.VMEM((SSD_GROUPS, SSD_STATE, 512), F32)],
        compiler_params=_cparams(("parallel", "parallel", "arbitrary")),
        name="ssd_scan",
    )(xbc, xbc, pr, dt_bias128, a_log128, s0)


def _fnet_chan_kernel(f_ref, cs_ref, o_ref):
    res = _dot(f_ref[...].astype(BF16), cs_ref[...])
    o_ref[0] = res[:, :FNET_GW].astype(o_ref.dtype)
    o_ref[1] = res[:, FNET_GW:].astype(o_ref.dtype)


def fnet_chan(pr, cs, tm):
    b, L, _ = pr.shape
    off = PR_F // FNET_GW
    return pl.pallas_call(
        _fnet_chan_kernel,
        grid=(b, L // tm, FNET_GROUPS),
        in_specs=[pl.BlockSpec((None, tm, FNET_GW), lambda bi, i, g: (bi, i, off + g)),
                  pl.BlockSpec((FNET_GW, 2 * FNET_GW), lambda bi, i, g: (0, 0))],
        out_specs=pl.BlockSpec((None, 2, tm, FNET_GW), lambda bi, i, g: (bi, 0, i, g)),
        out_shape=jax.ShapeDtypeStruct((b, 2, L, FNET_WIDTH), BF16),
        compiler_params=_cparams(("parallel", "parallel", "parallel")),
        name="fnet_chan",
    )(pr, cs)


def _fnet_pos_kernel(w_ref, x_ref, o_ref):
    o_ref[...] = _dot(w_ref[...], x_ref[...])


def fnet_pos(wpos, xcs, tm, tn):
    b, k, n = xcs.shape
    L = wpos.shape[0]
    return pl.pallas_call(
        _fnet_pos_kernel,
        grid=(b, n // tn, L // tm),
        in_specs=[pl.BlockSpec((tm, k), lambda bi, j, i: (i, 0)),
                  pl.BlockSpec((None, k, tn), lambda bi, j, i: (bi, 0, j))],
        out_specs=pl.BlockSpec((None, tm, tn), lambda bi, j, i: (bi, i, j)),
        out_shape=jax.ShapeDtypeStruct((b, L, n), F32),
        compiler_params=_cparams(("parallel", "parallel", "parallel")),
        name="fnet_pos",
    )(wpos, xcs)


def _dft_tables(n):
    k = np.arange(n, dtype=np.int64)
    ang = 2.0 * np.pi * ((k[:, None] * k[None, :]) % n).astype(np.float64) / n
    s = 1.0 / math.sqrt(n)
    return np.cos(ang) * s, np.sin(ang) * s


def _even_out_kernel(y0_ref, y1_ref, xs_ref, z_ref, f_ref, dsk_ref, nw_ref, w_ref, xres_ref, gate_ref,
                     lng_ref, lnb_ref, sc_ref, sh_ref, xo_ref, h2_ref, lhs_ref):
    y = y0_ref[...] + y1_ref[...] + dsk_ref[...] * xs_ref[...]
    u = y * _silu(z_ref[...])
    gw = SSD_WIDTH // SSD_GROUPS
    for g in range(SSD_GROUPS):
        ug = u[:, g * gw:(g + 1) * gw]
        ms = jnp.mean(ug * ug, axis=-1, keepdims=True)
        lhs_ref[:, g * gw:(g + 1) * gw] = (ug * lax.rsqrt(ms + LN_EPS) * nw_ref[:, g * gw:(g + 1) * gw]).astype(BF16)
    lhs_ref[:, SSD_WIDTH:] = f_ref[...].astype(BF16)
    ymix = _dot(lhs_ref[...], w_ref[...])
    xn = _layer_norm(ALPHA * xres_ref[...] + gate_ref[...] * ymix, lng_ref[...], lnb_ref[...])
    xo_ref[...] = xn
    h2_ref[...] = xn * (1.0 + sc_ref[...]) + sh_ref[...]


def even_out(y2, xbc, pr, fmix, dskip, norm_w, w_out, x_res, gate, ln_g, ln_b, sc2, sh2, tm):
    b, L, D = x_res.shape
    vec = lambda: pl.BlockSpec((1, D), lambda bi, i: (0, 0))
    bvec = lambda: pl.BlockSpec((None, 1, D), lambda bi, i: (bi, 0, 0))
    return pl.pallas_call(
        _even_out_kernel,
        grid=(b, L // tm),
        in_specs=[pl.BlockSpec((None, None, tm, SSD_WIDTH), lambda bi, i: (0, bi, i, 0)),
                  pl.BlockSpec((None, None, tm, SSD_WIDTH), lambda bi, i: (1, bi, i, 0)),
                  pl.BlockSpec((None, tm, SSD_WIDTH), lambda bi, i: (bi, i, 0)),
                  pl.BlockSpec((None, tm, SSD_WIDTH), lambda bi, i: (bi, i, PR_Z // SSD_WIDTH)),
                  pl.BlockSpec((None, tm, FNET_WIDTH), lambda bi, i: (bi, i, 0)),
                  pl.BlockSpec((1, SSD_WIDTH), lambda bi, i: (0, 0)),
                  pl.BlockSpec((1, SSD_WIDTH), lambda bi, i: (0, 0)),
                  pl.BlockSpec((D, D), lambda bi, i: (0, 0)),
                  pl.BlockSpec((None, tm, D), lambda bi, i: (bi, i, 0)),
                  bvec(), vec(), vec(), bvec(), bvec()],
        out_specs=[pl.BlockSpec((None, tm, D), lambda bi, i: (bi, i, 0)),
                   pl.BlockSpec((None, tm, D), lambda bi, i: (bi, i, 0))],
        out_shape=[jax.ShapeDtypeStruct((b, L, D), F32), jax.ShapeDtypeStruct((b, L, D), F32)],
        scratch_shapes=[pltpu.VMEM((tm, D), BF16)],
        compiler_params=_cparams(("parallel", "parallel")),
        name="even_out",
    )(y2, y2, xbc, pr, fmix, dskip, norm_w, w_out, x_res, gate, ln_g, ln_b, sc2, sh2)


def _router_kernel(h_ref, w_ref, bias_ref, o_ref):
    logits = _dot_x3(h_ref[...], w_ref[...])
    s = jax.nn.sigmoid(logits)
    lane = lax.broadcasted_iota(jnp.int32, s.shape, 1)
    neg = jnp.float32(-jnp.inf)
    ssel = s + bias_ref[...]
    big = jnp.int32(4 * LANES)

    def top2(mask):
        v = jnp.where(mask, ssel, neg)
        m1 = jnp.max(v, axis=-1, keepdims=True)
        i1 = jnp.min(jnp.where(v == m1, lane, big), axis=-1, keepdims=True)
        v2 = jnp.where(lane == i1, neg, v)
        m2 = jnp.max(v2, axis=-1, keepdims=True)
        i2 = jnp.min(jnp.where(v2 == m2, lane, big), axis=-1, keepdims=True)
        return m1 + m2, i1, i2

    best, bi1, bi2 = top2(lane < EXPERTS_PER_GROUP)
    for g in range(1, N_GROUPS):
        sc, i1, i2 = top2((lane >= g * EXPERTS_PER_GROUP) & (lane < (g + 1) * EXPERTS_PER_GROUP))
        better = sc > best
        best = jnp.where(better, sc, best)
        bi1 = jnp.where(better, i1, bi1)
        bi2 = jnp.where(better, i2, bi2)
    s1 = jnp.sum(jnp.where(lane == bi1, s, 0.0), axis=-1, keepdims=True)
    s2 = jnp.sum(jnp.where(lane == bi2, s, 0.0), axis=-1, keepdims=True)
    tot = s1 + s2
    out = jnp.where(lane == 0, bi1.astype(F32),
                    jnp.where(lane == 1, bi2.astype(F32),
                              jnp.where(lane == 2, s1 / tot, jnp.where(lane == 3, s2 / tot, 0.0))))
    o_ref[...] = out


def router(h, w128, bias128, tm):
    T, D = h.shape
    return pl.pallas_call(
        _router_kernel,
        grid=(T // tm,),
        in_specs=[pl.BlockSpec((tm, D), lambda i: (i, 0)),
                  pl.BlockSpec((D, LANES), lambda i: (0, 0)),
                  pl.BlockSpec((1, LANES), lambda i: (0, 0))],
        out_specs=pl.BlockSpec((tm, LANES), lambda i: (i, 0)),
        out_shape=jax.ShapeDtypeStruct((T, LANES), F32),
        compiler_params=_cparams(("parallel",)),
        name="router",
    )(h, w128, bias128)


def _moe_kernel(be_ref, nused_ref, src_ref, h_hbm, wg_ref, wu_ref, wd_ref, o_ref,
                xbuf, wgb, wub, wdb, sem):
    i = pl.program_id(0)
    bm = xbuf.shape[0]

    @pl.when(i < nused_ref[0])
    def _():
        def issue(r, carry):
            pltpu.make_async_copy(h_hbm.at[pl.ds(src_ref[0, 0, r], 1)], xbuf.at[pl.ds(r, 1)], sem).start()
            return carry

        lax.fori_loop(0, bm, issue, 0)

        prev = be_ref[jnp.maximum(i - 1, 0)]

        @pl.when((i == 0) | (be_ref[i] != prev))
        def _():
            wgb[...] = wg_ref[...].astype(BF16)
            wub[...] = wu_ref[...].astype(BF16)
            wdb[...] = wd_ref[...].astype(BF16)

        pltpu.make_async_copy(h_hbm.at[pl.ds(0, bm)], xbuf, sem).wait()
        x = xbuf[...].astype(BF16)
        hid = _silu(_dot(x, wgb[...])) * _dot(x, wub[...])
        o_ref[...] = _dot(hid.astype(BF16), wdb[...])

    @pl.when(i >= nused_ref[0])
    def _():
        o_ref[...] = jnp.zeros_like(o_ref)


def moe_experts(h, block_e, nused, src_tok, w_gate, w_up, w_down):
    T, D = h.shape
    nblk = block_e.shape[0]
    bm = MOE_BM
    gs = pltpu.PrefetchScalarGridSpec(
        num_scalar_prefetch=2,
        grid=(nblk,),
        in_specs=[pl.BlockSpec((1, 1, bm), lambda i, be, nu: (i, 0, 0), memory_space=pltpu.SMEM),
                  pl.BlockSpec(memory_space=pl.ANY),
                  pl.BlockSpec((None, D, D_EXPERT), lambda i, be, nu: (be[i], 0, 0)),
                  pl.BlockSpec((None, D, D_EXPERT), lambda i, be, nu: (be[i], 0, 0)),
                  pl.BlockSpec((None, D_EXPERT, D), lambda i, be, nu: (be[i], 0, 0))],
        out_specs=pl.BlockSpec((bm, D), lambda i, be, nu: (i, 0)),
        scratch_shapes=[pltpu.VMEM((bm, D), F32),
                        pltpu.VMEM((D, D_EXPERT), BF16),
                        pltpu.VMEM((D, D_EXPERT), BF16),
                        pltpu.VMEM((D_EXPERT, D), BF16),
                        pltpu.SemaphoreType.DMA(())],
    )
    return pl.pallas_call(
        _moe_kernel,
        grid_spec=gs,
        out_shape=jax.ShapeDtypeStruct((nblk * bm, D), F32),
        compiler_params=_cparams(("arbitrary",)),
        name="moe_experts",
    )(block_e, nused, src_tok.reshape(nblk, 1, bm), h, w_gate, w_up, w_down)


def _moe_combine_kernel(dest_ref, yb_hbm, rw_ref, x_ref, gate_ref, lng_ref, lnb_ref, o_ref, ybuf, sem):
    tm = x_ref.shape[0]

    def issue(r, carry):
        pltpu.make_async_copy(yb_hbm.at[pl.ds(dest_ref[0, 0, r], 1)], ybuf.at[0, pl.ds(r, 1)], sem).start()
        pltpu.make_async_copy(yb_hbm.at[pl.ds(dest_ref[0, 0, tm + r], 1)], ybuf.at[1, pl.ds(r, 1)], sem).start()
        return carry

    lax.fori_loop(0, tm, issue, 0)
    pltpu.make_async_copy(yb_hbm.at[pl.ds(0, tm)], ybuf.at[0], sem).wait()
    pltpu.make_async_copy(yb_hbm.at[pl.ds(0, tm)], ybuf.at[1], sem).wait()
    rw = rw_ref[...]
    y2 = rw[:, 2:3] * ybuf[0] + rw[:, 3:4] * ybuf[1]
    o_ref[...] = _layer_norm(ALPHA * x_ref[...] + gate_ref[...] * y2, lng_ref[...], lnb_ref[...])


def moe_combine(dest_tiles, ybuf, rw, x, gates, gate_tile_map, ln_g, ln_b, tm):
    T, D = x.shape
    return pl.pallas_call(
        _moe_combine_kernel,
        grid=(T // tm,),
        in_specs=[pl.BlockSpec((1, 1, 2 * tm), lambda i: (i, 0, 0), memory_space=pltpu.SMEM),
                  pl.BlockSpec(memory_space=pl.ANY),
                  pl.BlockSpec((tm, LANES), lambda i: (i, 0)),
                  pl.BlockSpec((tm, D), lambda i: (i, 0)),
                  pl.BlockSpec((None, 1, D), lambda i: (gate_tile_map(i), 0, 0)),
                  pl.BlockSpec((1, D), lambda i: (0, 0)),
                  pl.BlockSpec((1, D), lambda i: (0, 0))],
        out_specs=pl.BlockSpec((tm, D), lambda i: (i, 0)),
        out_shape=jax.ShapeDtypeStruct((T, D), F32),
        scratch_shapes=[pltpu.VMEM((2, tm, D), F32), pltpu.SemaphoreType.DMA(())],
        compiler_params=_cparams(("arbitrary",)),
        name="moe_combine",
    )(dest_tiles, ybuf, rw, x, gates, ln_g, ln_b)


def moe_layer(h2, x_res, gates, gate_tile_map, router_w128, router_b128, w_gate, w_up, w_down, ln_g, ln_b, tm=256):
    T, D = h2.shape
    bm = MOE_BM
    rw = router(h2, router_w128, router_b128, tm)
    idx = rw[:, :TOP_K].astype(jnp.int32)
    e_flat = idx.reshape(-1)
    A = e_flat.shape[0]
    onehot = (e_flat[:, None] == jnp.arange(N_EXPERTS, dtype=jnp.int32)[None, :]).astype(jnp.int32)
    csum = jnp.cumsum(onehot, axis=0)
    counts = csum[-1]
    rank = jnp.sum(onehot * csum, axis=1) - 1
    padded = (counts + bm - 1) // bm * bm
    ends = jnp.cumsum(padded)
    pstart = ends - padded
    dest = pstart[e_flat] + rank
    nblk = (A + N_EXPERTS * (bm - 1) + bm - 1) // bm
    P = nblk * bm
    tok = jnp.arange(A, dtype=jnp.int32) // TOP_K
    src_tok = jnp.zeros((P,), jnp.int32).at[dest].set(tok)
    block_e = jnp.minimum(jnp.searchsorted(ends, jnp.arange(nblk, dtype=jnp.int32) * bm, side="right"),
                          N_EXPERTS - 1).astype(jnp.int32)
    nused = (ends[-1] // bm).astype(jnp.int32).reshape(1)
    ybuf = moe_experts(h2, block_e, nused, src_tok, w_gate, w_up, w_down)
    dest2 = dest.reshape(T // tm, tm, TOP_K)
    dest_tiles = jnp.concatenate([dest2[:, :, 0], dest2[:, :, 1]], axis=1).reshape(T // tm, 1, 2 * tm)
    return moe_combine(dest_tiles.astype(jnp.int32), ybuf, rw, x_res, gates, gate_tile_map, ln_g, ln_b, tm)


def _rwkv_proj_kernel(x_ref, xp_ref, xn_ref, sc_ref, sh_ref, mu_ref, w_ref, o_ref, h_ref, xx_ref, mix_ref,
                      *, grid_w, n_main):
    i = pl.program_id(1)
    j = pl.program_id(2)
    nt = pl.num_programs(1)
    tm, D = h_ref.shape
    q = D // 4

    @pl.when(j == 0)
    def _():
        sc = 1.0 + sc_ref[...]
        sh = sh_ref[...]
        h = x_ref[...] * sc + sh
        h_ref[...] = h
        row = lax.broadcasted_iota(jnp.int32, (tm, q), 0)
        if grid_w is None:
            prev = lambda a: jnp.where(row == 0, 0.0, pltpu.roll(a, 1, axis=0))
            nxt = lambda a: jnp.where(row == tm - 1, 0.0, pltpu.roll(a, tm - 1, axis=0))
            parts = [prev(h[:, 0:q]), nxt(h[:, q:2 * q]), prev(h[:, 2 * q:3 * q]), nxt(h[:, 3 * q:])]
        else:
            wpos = row % grid_w
            left = jnp.where(wpos == 0, 0.0, pltpu.roll(h[:, 0:q], 1, axis=0))
            right = jnp.where(wpos == grid_w - 1, 0.0, pltpu.roll(h[:, q:2 * q], tm - 1, axis=0))
            hp = xp_ref[:, 2 * q:3 * q] * sc[:, 2 * q:3 * q] + sh[:, 2 * q:3 * q]
            hn = xn_ref[:, 3 * q:] * sc[:, 3 * q:] + sh[:, 3 * q:]
            hp = jnp.where(i == 0, 0.0, hp)
            hn = jnp.where(i == nt - 1, 0.0, hn)
            up = jnp.concatenate([hp, h[:tm - grid_w, 2 * q:3 * q]], axis=0)
            down = jnp.concatenate([h[grid_w:, 3 * q:], hn], axis=0)
            parts = [left, right, up, down]
        for k in range(4):
            xx_ref[:, k * q:(k + 1) * q] = parts[k] - h[:, k * q:(k + 1) * q]

    is_main = j < 3 * n_main
    new_stream = jnp.where(is_main, j % n_main == 0, True)

    @pl.when(new_stream)
    def _():
        mix_ref[...] = (h_ref[...] + xx_ref[...] * mu_ref[...]).astype(BF16)

    o_ref[...] = _dot(mix_ref[...], w_ref[...]).astype(o_ref.dtype)


def rwkv_proj(x, sc, sh, mu6, wcat, tm, tn, grid_w):
    b, L, D = x.shape
    n = wcat.shape[1]
    n_main = D // tn
    nj = n // tn
    gw = 64 if grid_w is None else grid_w
    nh = L // gw

    def stream(j):
        return jnp.where(j < 3 * n_main, j // n_main, 3 + (j - 3 * n_main) // (LORA_PAD // tn))

    kern = functools.partial(_rwkv_proj_kernel, grid_w=grid_w, n_main=n_main)
    return pl.pallas_call(
        kern,
        grid=(b, L // tm, nj),
        in_specs=[pl.BlockSpec((None, tm, D), lambda bi, i, j: (bi, i, 0)),
                  pl.BlockSpec((None, gw, D), lambda bi, i, j: (bi, jnp.maximum(i * (tm // gw) - 1, 0), 0)),
                  pl.BlockSpec((None, gw, D), lambda bi, i, j: (bi, jnp.minimum((i + 1) * (tm // gw), nh - 1), 0)),
                  pl.BlockSpec((None, 1, D), lambda bi, i, j: (bi, 0, 0)),
                  pl.BlockSpec((None, 1, D), lambda bi, i, j: (bi, 0, 0)),
                  pl.BlockSpec((None, 1, D), lambda bi, i, j: (stream(j), 0, 0)),
                  pl.BlockSpec((D, tn), lambda bi, i, j: (0, j))],
        out_specs=pl.BlockSpec((None, tm, tn), lambda bi, i, j: (bi, i, j)),
        out_shape=jax.ShapeDtypeStruct((b, L, n), BF16),
        scratch_shapes=[pltpu.VMEM((tm, D), F32), pltpu.VMEM((tm, D), F32), pltpu.VMEM((tm, D), BF16)],
        compiler_params=_cparams(("parallel", "parallel", "arbitrary")),
        name="rwkv_proj",
    )(x, x, x, sc, sh, mu6, wcat)


def _seg_sum64(x, ones_blk):
    hi, lo = _split2(x)
    return _dot(hi, ones_blk) + _dot(lo, ones_blk)


def _rwkv_post_kernel(r_ref, k_ref, v_ref, hw_ref, ha_ref, hg_ref, w2_ref, a2_ref, g2_ref,
                      w0_ref, a0_ref, kk_ref, ka_ref, rk_ref,
                      ro_ref, kko_ref, vo_ref, bo_ref, go_ref, lw_ref, kd_ref, kao_ref):
    r = r_ref[...].astype(F32)
    k = k_ref[...].astype(F32)
    v = v_ref[...].astype(F32)
    tm, D = r.shape
    li = lax.broadcasted_iota(jnp.int32, (LANES, LANES), 0) // RWKV_HEADSIZE
    lj = lax.broadcasted_iota(jnp.int32, (LANES, LANES), 1) // RWKV_HEADSIZE
    ones_blk = jnp.where(li == lj, 1.0, 0.0).astype(BF16)
    g = _dot(jax.nn.sigmoid(hg_ref[...].astype(F32)).astype(BF16), g2_ref[...])
    thw = jnp.tanh(hw_ref[...].astype(F32)).astype(BF16)
    ha = ha_ref[...]
    kkr = k * kk_ref[...]
    rkr = r * k * rk_ref[...]
    aas = []
    for e in range(2):
        dw = _dot(thw, w2_ref[e])
        wl = -_softplus(-(w0_ref[e] + dw)) - 0.5
        lw = -jnp.exp(wl)
        a = jax.nn.sigmoid(a0_ref[e] + _dot(ha, a2_ref[e]))
        aas.append(a)
        kd = k * (1.0 + (a - 1.0) * ka_ref[...])
        for p in range(RWKV_PAIRS):
            sl = slice(p * LANES, (p + 1) * LANES)
            lw_ref[e, p] = lw[:, sl]
            kd_ref[e, p] = kd[:, sl].astype(kd_ref.dtype)
    for p in range(RWKV_PAIRS):
        sl = slice(p * LANES, (p + 1) * LANES)
        kp = kkr[:, sl]
        nrm = jnp.maximum(jnp.sqrt(_seg_sum64(kp * kp, ones_blk)), 1e-12)
        kkp = kp / nrm
        kko_ref[p] = kkp.astype(kko_ref.dtype)
        for e in range(2):
            kao_ref[e, p] = (kkp * aas[e][:, sl]).astype(kao_ref.dtype)
        bo_ref[p] = (_seg_sum64(rkr[:, sl], ones_blk) * v[:, sl]).astype(bo_ref.dtype)
        ro_ref[p] = r[:, sl].astype(ro_ref.dtype)
        vo_ref[p] = v[:, sl].astype(vo_ref.dtype)
        go_ref[p] = g[:, sl].astype(go_ref.dtype)


def rwkv_post(proj, w2z, a2z, g2, w0, a0, k_k, k_a, r_k, tm):
    b, L, _ = proj.shape
    D = D_MODEL
    NP = RWKV_PAIRS
    col = lambda off, w: pl.BlockSpec((None, tm, w), lambda bi, i: (bi, i, off // w))
    full = lambda shape: pl.BlockSpec(shape, lambda bi, i: (0,) * len(shape))
    pair = lambda: pl.BlockSpec((None, NP, tm, LANES), lambda bi, i: (bi, 0, i, 0))
    pair2 = lambda: pl.BlockSpec((2, None, NP, tm, LANES), lambda bi, i: (0, bi, 0, i, 0))
    sh1 = lambda dt: jax.ShapeDtypeStruct((b, NP, L, LANES), dt)
    sh2 = lambda dt: jax.ShapeDtypeStruct((2, b, NP, L, LANES), dt)
    return pl.pallas_call(
        _rwkv_post_kernel,
        grid=(b, L // tm),
        in_specs=[col(0, D), col(D, D), col(2 * D, D),
                  col(3 * D, LORA_PAD), col(3 * D + LORA_PAD, LORA_PAD), col(3 * D + 2 * LORA_PAD, LORA_PAD),
                  full((2, LORA_PAD, D)), full((2, LORA_PAD, D)), full((LORA_PAD, D)),
                  full((2, 1, D)), full((2, 1, D)), full((1, D)), full((1, D)), full((1, D))],
        out_specs=[pair(), pair(), pair(), pair(), pair(), pair2(), pair2(), pair2()],
        out_shape=[sh1(BF16), sh1(BF16), sh1(BF16), sh1(BF16), sh1(BF16), sh2(F32), sh2(BF16), sh2(BF16)],
        compiler_params=_cparams(("parallel", "parallel")),
        name="rwkv_post",
    )(proj, proj, proj, proj, proj, proj, w2z, a2z, g2, w0, a0, k_k, k_a, r_k)


def _wkv_kernel(r_ref, kk_ref, v_ref, lw_ref, kd_ref, ka_ref, s0_ref, y_ref, sfin_ref, st_ref):
    T = WKV_T
    d = pl.program_id(0)
    c = pl.program_id(2)
    nc = pl.num_programs(2)
    fwd = d == 0

    @pl.when(c == 0)
    def _():
        st_ref[...] = s0_ref[...]

    row = lax.broadcasted_iota(jnp.int32, (T, T), 0)
    col = lax.broadcasted_iota(jnp.int32, (T, T), 1)
    sgn = 1 - 2 * d
    tri = jnp.where((row - col) * sgn >= 0, 1.0, 0.0).astype(BF16)
    row2 = lax.broadcasted_iota(jnp.int32, (T, 2 * T), 0)
    col2 = lax.broadcasted_iota(jnp.int32, (T, 2 * T), 1) % T
    incl2 = (row2 - col2) * sgn >= 0
    strict2 = (row2 - col2) * sgn > 0
    lane = lax.broadcasted_iota(jnp.int32, (T, LANES), 1)
    m_a = lane < RWKV_HEADSIZE
    bi = lax.broadcasted_iota(jnp.int32, (LANES, LANES), 0) // RWKV_HEADSIZE
    bj = lax.broadcasted_iota(jnp.int32, (LANES, LANES), 1) // RWKV_HEADSIZE
    blockdiag = bi == bj
    eye2 = jnp.where(row2 == col2, 1.0, 0.0)
    blk = []
    size = 8
    while size <= T:
        blk.append(row2 // size == col2 // size)
        size *= 2

    def stack2(z):
        zero = jnp.zeros_like(z)
        return jnp.concatenate([jnp.where(m_a, z, zero), jnp.where(m_a, zero, z)], axis=0)

    def pairmul(xm, ym):
        return _dot_x3(xm, stack2(ym))

    def body(p, carry):
        lw = lw_ref[p]
        cum = _dot_exact_lhs(tri, lw)
        c_end = jnp.where(fwd, cum[T - 1:T], cum[0:1])
        e_pos = jnp.exp(cum)
        e_neg = jnp.exp(-cum)
        kk = kk_ref[p].astype(F32)
        at = -kk * jnp.exp(cum - lw)
        rt = r_ref[p].astype(F32) * e_pos
        bt = (ka_ref[p].astype(F32) * e_neg).astype(BF16)
        kt = (kd_ref[p].astype(F32) * e_neg).astype(BF16)
        v = v_ref[p].astype(F32)
        x1 = jnp.concatenate([at, rt], axis=0).astype(BF16)
        x2s = jnp.concatenate([stack2(bt), stack2(kt)], axis=0)
        gall = _dot_nt(x1, x2s)
        lc = jnp.where(strict2, gall[:T, :2 * T], 0.0)
        lak = jnp.where(strict2, gall[:T, 2 * T:], 0.0)
        rb = jnp.where(incl2, gall[T:, :2 * T], 0.0)
        rk = jnp.where(incl2, gall[T:, 2 * T:], 0.0)
        n1 = jnp.where(blk[0], lc, 0.0)
        n2 = pairmul(n1, n1)
        n4 = pairmul(n2, n2)
        inv = eye2 + n1
        inv = inv + pairmul(inv, n2)
        inv = inv + pairmul(inv, n4)
        for lvl in range(1, len(blk)):
            off = jnp.where(blk[lvl] & jnp.logical_not(blk[lvl - 1]), lc, 0.0)
            inv = inv + pairmul(inv, pairmul(off, inv))
        s = st_ref[p]
        h = _dot_nt(x1, s.astype(BF16))
        v2 = stack2(v).astype(BF16)
        wv = h[:T] + _dot(lak.astype(BF16), v2)
        u = _dot_x3(inv, stack2(wv))
        rbk = jnp.concatenate([rb, rk], axis=1).astype(BF16)
        uv2 = jnp.concatenate([stack2(u).astype(BF16), v2], axis=0)
        y_ref[p] = (h[T:] + _dot(rbk, uv2)).astype(y_ref.dtype)
        uv = jnp.concatenate([u, v], axis=0).astype(BF16)
        bk = jnp.concatenate([bt, kt], axis=0)
        s_new = jnp.where(blockdiag, s + _dot_tn(uv, bk), 0.0) * jnp.exp(c_end)
        st_ref[p] = s_new
        return carry

    lax.fori_loop(0, RWKV_PAIRS, body, 0)

    @pl.when(c == nc - 1)
    def _():
        sfin_ref[...] = st_ref[...]


def wkv_scan(r, kk, v, lw, kd, ka, s0):
    b, NP, L, _ = r.shape
    T = WKV_T
    nc = L // T

    def cidx(d, c):
        return jnp.where(d == 0, c, nc - 1 - c)

    shared = lambda: pl.BlockSpec((None, NP, T, LANES), lambda d, bi, c: (bi, 0, cidx(d, c), 0))
    perdir = lambda: pl.BlockSpec((None, None, NP, T, LANES), lambda d, bi, c: (d, bi, 0, cidx(d, c), 0))
    state = lambda: pl.BlockSpec((None, None, NP, LANES, LANES), lambda d, bi, c: (d, bi, 0, 0, 0))
    return pl.pallas_call(
        _wkv_kernel,
        grid=(2, b, nc),
        in_specs=[shared(), shared(), shared(), perdir(), perdir(), perdir(), state()],
        out_specs=[perdir(), state()],
        out_shape=[jax.ShapeDtypeStruct((2, b, NP, L, LANES), F32),
                   jax.ShapeDtypeStruct((2, b, NP, LANES, LANES), F32)],
        scratch_shapes=[pltpu.VMEM((NP, LANES, LANES), F32)],
        compiler_params=_cparams(("parallel", "parallel", "arbitrary")),
        name="wkv_scan",
    )(r, kk, v, lw, kd, ka, s0)


def _odd_out_kernel(y0_ref, y1_ref, bo_ref, g_ref, lw_ref, lb_ref, w_ref, xres_ref, gate_ref,
                    lng_ref, lnb_ref, sc_ref, sh_ref, xo_ref, h2_ref, lhs_ref):
    li = lax.broadcasted_iota(jnp.int32, (LANES, LANES), 0) // RWKV_HEADSIZE
    lj = lax.broadcasted_iota(jnp.int32, (LANES, LANES), 1) // RWKV_HEADSIZE
    ones_blk = jnp.where(li == lj, 1.0, 0.0).astype(BF16)
    inv = 1.0 / RWKV_HEADSIZE
    for p in range(RWKV_PAIRS):
        sl = slice(p * LANES, (p + 1) * LANES)
        y = y0_ref[p] + y1_ref[p]
        mu = _seg_sum64(y, ones_blk) * inv
        dlt = y - mu
        var = _seg_sum64(dlt * dlt, ones_blk) * inv
        yn = dlt * lax.rsqrt(var + GN_EPS) * lw_ref[:, sl] + lb_ref[:, sl]
        lhs_ref[:, sl] = ((yn + bo_ref[p].astype(F32)) * g_ref[p].astype(F32)).astype(BF16)
    ymix = _dot(lhs_ref[...], w_ref[...])
    xn = _layer_norm(ALPHA * xres_ref[...] + gate_ref[...] * ymix, lng_ref[...], lnb_ref[...])
    xo_ref[...] = xn
    h2_ref[...] = xn * (1.0 + sc_ref[...]) + sh_ref[...]


def odd_out(y2, bonus, g, lnx_w, lnx_b, w_o, x_res, gate, ln_g, ln_b, sc2, sh2, tm):
    b, L, D = x_res.shape
    NP = RWKV_PAIRS
    vec = lambda: pl.BlockSpec((1, D), lambda bi, i: (0, 0))
    bvec = lambda: pl.BlockSpec((None, 1, D), lambda bi, i: (bi, 0, 0))
    pair = lambda: pl.BlockSpec((None, NP, tm, LANES), lambda bi, i: (bi, 0, i, 0))
    return pl.pallas_call(
        _odd_out_kernel,
        grid=(b, L // tm),
        in_specs=[pl.BlockSpec((None, None, NP, tm, LANES), lambda bi, i: (0, bi, 0, i, 0)),
                  pl.BlockSpec((None, None, NP, tm, LANES), lambda bi, i: (1, bi, 0, i, 0)),
                  pair(), pair(), vec(), vec(),
                  pl.BlockSpec((D, D), lambda bi, i: (0, 0)),
                  pl.BlockSpec((None, tm, D), lambda bi, i: (bi, i, 0)),
                  bvec(), vec(), vec(), bvec(), bvec()],
        out_specs=[pl.BlockSpec((None, tm, D), lambda bi, i: (bi, i, 0)),
                   pl.BlockSpec((None, tm, D), lambda bi, i: (bi, i, 0))],
        out_shape=[jax.ShapeDtypeStruct((b, L, D), F32), jax.ShapeDtypeStruct((b, L, D), F32)],
        scratch_shapes=[pltpu.VMEM((tm, D), BF16)],
        compiler_params=_cparams(("parallel", "parallel")),
        name="odd_out",
    )(y2, y2, bonus, g, lnx_w, lnx_b, w_o, x_res, gate, ln_g, ln_b, sc2, sh2)


def _pad_cols(w, n):
    return jnp.pad(w, ((0, 0), (0, n - w.shape[1])))


def even_layer(x_lat, x_ctx, ml, mc, p):
    bsz = x_lat.shape[0]
    o1, o2, o3 = SSD_WIDTH, SSD_WIDTH + SSD_CONV_DIM, SSD_WIDTH + SSD_CONV_DIM + 2 * SSD_HEADS
    w_in = p["w_in"]
    w_pad = jnp.concatenate([w_in[:, :o1], w_in[:, o1:o2], w_in[:, o3:],
                             _pad_cols(w_in[:, o2:o3], PR_N - PR_DT)], axis=1).astype(BF16)
    dt_bias128 = _pad_cols(p["dt_bias"].reshape(1, 2 * SSD_HEADS), LANES)
    a_log128 = _pad_cols(p["a_log"].reshape(1, 2 * SSD_HEADS), LANES)
    dskip = jnp.repeat(p["d_skip"], SSD_HEADDIM).reshape(1, SSD_WIDTH)
    norm_w = p["norm_w"].reshape(1, SSD_WIDTH)
    conv_b = p["conv_b"].reshape(1, SSD_CONV_DIM)
    w_out = p["w_out"].astype(BF16)
    cc, sc = _dft_tables(FNET_GW)
    cs = jnp.asarray(np.concatenate([cc, sc], axis=1), BF16)

    def run(x, m, s0):
        L = x.shape[1]
        tm = min(L, 512)
        pr = modmm(x, m[:, 1], m[:, 0], w_pad, tm, 768)
        xbc = conv_silu(pr, p["conv_w"], conv_b)
        y2, s_fin = ssd_scan(xbc, pr, dt_bias128, a_log128, s0)
        cl, sl = _dft_tables(L)
        wpos = jnp.asarray(np.concatenate([cl, -sl], axis=1), BF16)
        xcs = fnet_chan(pr, cs, tm).reshape(bsz, 2 * L, FNET_WIDTH)
        fmix = fnet_pos(wpos, xcs, min(L, 512), 512)
        x_new, h2 = even_out(y2, xbc, pr, fmix, dskip, norm_w, w_out, x, m[:, 2], p["ln_g"], p["ln_b"],
                             m[:, 4], m[:, 3], min(L, 256))
        return x_new, h2, s_fin

    s0 = jnp.zeros((2, bsz, SSD_GROUPS, SSD_STATE, SSD_HPG * SSD_HEADDIM), F32)
    xc_new, h2c, s_ctx = run(x_ctx, mc, s0)
    xl_new, h2l, _ = run(x_lat, ml, s_ctx)
    return xl_new, h2l, xc_new, h2c


def odd_layer(x_lat, x_ctx, ml, mc, p, need_ctx):
    bsz = x_lat.shape[0]
    D = D_MODEL
    mu = p["mu"]
    mu6 = jnp.stack([mu[0], mu[2], mu[3], mu[1], mu[4], mu[5]]).reshape(6, 1, D)
    w_rkv = p["w_rkv"]
    w1cat = _pad_cols(jnp.concatenate([p["w1"][0], p["w1"][1]], axis=1), LORA_PAD)
    a1cat = _pad_cols(jnp.concatenate([p["a1"][0], p["a1"][1]], axis=1), LORA_PAD)
    wcat = jnp.concatenate([w_rkv[0], w_rkv[1], w_rkv[2], w1cat, a1cat, p["g1"]], axis=1).astype(BF16)

    def lora2(w2):
        z = jnp.zeros((2, LORA_PAD, D), F32)
        z = z.at[0, 0:w2.shape[1]].set(w2[0])
        z = z.at[1, w2.shape[1]:2 * w2.shape[1]].set(w2[1])
        return z.astype(BF16)

    w2z = lora2(p["w2"])
    a2z = lora2(p["a2"])
    g2 = p["g2"].astype(BF16)
    w0 = p["w0"].reshape(2, 1, D)
    a0 = p["a0"].reshape(2, 1, D)
    k_k = p["k_k"].reshape(1, D)
    k_a = p["k_a"].reshape(1, D)
    r_k = p["r_k"].reshape(1, D)
    lnx_w = p["lnx_w"].reshape(1, D)
    lnx_b = p["lnx_b"].reshape(1, D)
    w_o = p["w_o"].astype(BF16)

    def run(x, m, s0, grid_w, need_out):
        L = x.shape[1]
        tm = min(L, 512)
        proj = rwkv_proj(x, m[:, 1], m[:, 0], mu6, wcat, tm, 256, grid_w)
        r, kk, v, bonus, g, lw, kd, ka = rwkv_post(proj, w2z, a2z, g2, w0, a0, k_k, k_a, r_k, min(L, 256))
        y2, s_fin = wkv_scan(r, kk, v, lw, kd, ka, s0)
        if not need_out:
            return None, None, s_fin
        x_new, h2 = odd_out(y2, bonus, g, lnx_w, lnx_b, w_o, x, m[:, 2], p["ln_g"], p["ln_b"],
                            m[:, 4], m[:, 3], min(L, 256))
        return x_new, h2, s_fin

    s0 = jnp.zeros((2, bsz, RWKV_PAIRS, LANES, LANES), F32)
    xc_new, h2c, s_ctx = run(x_ctx, mc, s0, None, need_ctx)
    xl_new, h2l, _ = run(x_lat, ml, s_ctx, 64, True)
    return xl_new, h2l, xc_new, h2c


def kernel(x, c, ctx, c_ctx, w_mod, b_mod, ln_g, ln_b, ssd_w_in, ssd_conv_w, ssd_conv_b, ssd_a_log, ssd_dt_bias, ssd_d, ssd_norm_w, even_w_out, rwkv_mu, rwkv_w_rkv, rwkv_w_o, rwkv_w0, rwkv_w1, rwkv_w2, rwkv_a0, rwkv_a1, rwkv_a2, rwkv_g1, rwkv_g2, rwkv_k_k, rwkv_k_a, rwkv_r_k, rwkv_lnx_w, rwkv_lnx_b, router_w, router_bias, moe_w_gate, moe_w_up, moe_w_down):
    bsz, L, D = x.shape
    Lc = ctx.shape[1]
    n_lat = bsz * L
    router_w128 = _pad_cols(router_w, LANES)
    router_b128 = jnp.pad(router_bias.reshape(1, N_EXPERTS), ((0, 0), (0, LANES - N_EXPERTS)),
                          constant_values=-jnp.inf)
    cc = jnp.concatenate([c, c_ctx[None, :], jnp.zeros((7, D), F32)], axis=0)
    x_lat, x_ctx = x, ctx
    for i in range(DEPTH):
        last = i == DEPTH - 1
        j = i // 2
        m_all = dense_silu(cc, w_mod[i], b_mod[i].reshape(1, 6 * D))
        ml = m_all[:bsz].reshape(bsz, 6, 1, D)
        mc = jnp.broadcast_to(m_all[bsz].reshape(1, 6, 1, D), (bsz, 6, 1, D))
        lng0, lnb0 = ln_g[i, 0].reshape(1, D), ln_b[i, 0].reshape(1, D)
        lng1, lnb1 = ln_g[i, 1].reshape(1, D), ln_b[i, 1].reshape(1, D)
        if i % 2 == 0:
            p = dict(w_in=ssd_w_in[j], conv_w=ssd_conv_w[j], conv_b=ssd_conv_b[j], a_log=ssd_a_log[j],
                     dt_bias=ssd_dt_bias[j], d_skip=ssd_d[j], norm_w=ssd_norm_w[j], w_out=even_w_out[j],
                     ln_g=lng0, ln_b=lnb0)
            xl, h2l, xc, h2c = even_layer(x_lat, x_ctx, ml, mc, p)
        else:
            p = dict(mu=rwkv_mu[j], w_rkv=rwkv_w_rkv[j], w_o=rwkv_w_o[j], w0=rwkv_w0[j], w1=rwkv_w1[j],
                     w2=rwkv_w2[j], a0=rwkv_a0[j], a1=rwkv_a1[j], a2=rwkv_a2[j], g1=rwkv_g1[j],
                     g2=rwkv_g2[j], k_k=rwkv_k_k[j], k_a=rwkv_k_a[j], r_k=rwkv_r_k[j],
                     lnx_w=rwkv_lnx_w[j], lnx_b=rwkv_lnx_b[j], ln_g=lng0, ln_b=lnb0)
            xl, h2l, xc, h2c = odd_layer(x_lat, x_ctx, ml, mc, p, not last)
        tm = 256
        if last:
            gates = ml[:, 5]
            out = moe_layer(h2l.reshape(n_lat, D), xl.reshape(n_lat, D), gates,
                            lambda t: t // (L // tm), router_w128, router_b128,
                            moe_w_gate[i], moe_w_up[i], moe_w_down[i], lng1, lnb1, tm)
            x_lat = out.reshape(bsz, L, D)
        else:
            h2 = jnp.concatenate([h2l.reshape(n_lat, D), h2c.reshape(bsz * Lc, D)], axis=0)
            xr = jnp.concatenate([xl.reshape(n_lat, D), xc.reshape(bsz * Lc, D)], axis=0)
            gates = jnp.concatenate([ml[:, 5], mc[:1, 5]], axis=0)
            n_lat_tiles = n_lat // tm
            out = moe_layer(h2, xr, gates,
                            lambda t: jnp.where(t < n_lat_tiles, t // (L // tm), bsz), router_w128, router_b128,
                            moe_w_gate[i], moe_w_up[i], moe_w_down[i], lng1, lnb1, tm)
            x_lat = out[:n_lat].reshape(bsz, L, D)
            x_ctx = out[n_lat:].reshape(bsz, Lc, D)
    return x_lat
```

```python
import functools
import math

import numpy as np
import jax
import jax.numpy as jnp
from jax import lax
from jax.experimental import pallas as pl
from jax.experimental.pallas import tpu as pltpu

F32 = jnp.float32
BF16 = jnp.bfloat16

V7X_VMEM_BYTES = 64 * 1024 * 1024
VMEM_LIMIT = V7X_VMEM_BYTES - 8 * 1024 * 1024
LANES = 128

D_MODEL = 2048
DEPTH = 2
ALPHA = (2 * DEPTH) ** 0.25
LN_EPS = 1e-5
GN_EPS = 64e-5

SSD_WIDTH = 1024
SSD_HEADDIM = 64
SSD_HEADS = 16
SSD_GROUPS = 2
SSD_HPG = 8
SSD_STATE = 128
SSD_CONV_DIM = SSD_WIDTH + 2 * SSD_GROUPS * SSD_STATE
FNET_WIDTH = 1024
FNET_GROUPS = 4
FNET_GW = 256
SSD_Q = 128
PR_Z, PR_XBC, PR_F, PR_DT, PR_N = 0, 1024, 2560, 3584, 3840

RWKV_HEADSIZE = 64
RWKV_HEADS = 32
RWKV_PAIRS = 16
DECAY_LORA = 96
AAA_LORA = 96
GATE_LORA = 256
LORA_PAD = 256
RW_N = 3 * D_MODEL + 3 * LORA_PAD
WKV_T = 64

N_EXPERTS = 32
N_GROUPS = 4
EXPERTS_PER_GROUP = 8
TOP_K = 2
D_EXPERT = 512
MOE_BM = 256


def _cparams(sem):
    return pltpu.CompilerParams(dimension_semantics=sem, vmem_limit_bytes=VMEM_LIMIT)


def _dot(a, b):
    return jnp.dot(a, b, preferred_element_type=F32)


def _dot_nt(a, b):
    return lax.dot_general(a, b, (((1,), (1,)), ((), ())), preferred_element_type=F32)


def _dot_tn(a, b):
    return lax.dot_general(a, b, (((0,), (0,)), ((), ())), preferred_element_type=F32)


def _split2(a):
    hi = a.astype(BF16)
    lo = (a - hi.astype(F32)).astype(BF16)
    return hi, lo


def _split3(a):
    hi = a.astype(BF16)
    r1 = a - hi.astype(F32)
    mid = r1.astype(BF16)
    lo = (r1 - mid.astype(F32)).astype(BF16)
    return hi, mid, lo


def _dot_exact_lhs(a_bf16, b_f32):
    b1, b2, b3 = _split3(b_f32)
    return _dot(a_bf16, b1) + _dot(a_bf16, b2) + _dot(a_bf16, b3)


def _dot_exact_rhs(a_f32, b_bf16):
    a1, a2, a3 = _split3(a_f32)
    return _dot(a1, b_bf16) + _dot(a2, b_bf16) + _dot(a3, b_bf16)


def _dot_x3(a_f32, b_f32):
    a1, a2 = _split2(a_f32)
    b1, b2 = _split2(b_f32)
    return _dot(a1, b1) + _dot(a1, b2) + _dot(a2, b1)


def _silu(x):
    return x * jax.nn.sigmoid(x)


def _softplus(x):
    return jnp.maximum(x, 0.0) + jnp.log(1.0 + jnp.exp(-jnp.abs(x)))


def _layer_norm(v, g, b):
    mu = jnp.mean(v, axis=-1, keepdims=True)
    d = v - mu
    var = jnp.mean(d * d, axis=-1, keepdims=True)
    return d * lax.rsqrt(var + LN_EPS) * g + b


def _dense_silu_kernel(x_ref, w_ref, b_ref, o_ref):
    x = _silu(x_ref[...]).astype(BF16)
    o_ref[...] = _dot(x, w_ref[...].astype(BF16)) + b_ref[...]


def dense_silu(x, w, bias, tn=512):
    m, k = x.shape
    n = w.shape[1]
    return pl.pallas_call(
        _dense_silu_kernel,
        grid=(n // tn,),
        in_specs=[pl.BlockSpec((m, k), lambda j: (0, 0)),
                  pl.BlockSpec((k, tn), lambda j: (0, j)),
                  pl.BlockSpec((1, tn), lambda j: (0, j))],
        out_specs=pl.BlockSpec((m, tn), lambda j: (0, j)),
        out_shape=jax.ShapeDtypeStruct((m, n), F32),
        compiler_params=_cparams(("parallel",)),
        name="dense_silu",
    )(x, w, bias)


def _modmm_kernel(x_ref, sc_ref, sh_ref, w_ref, o_ref, h_ref):
    @pl.when(pl.program_id(2) == 0)
    def _():
        h_ref[...] = (x_ref[...] * (1.0 + sc_ref[...]) + sh_ref[...]).astype(BF16)

    o_ref[...] = _dot(h_ref[...], w_ref[...]).astype(o_ref.dtype)


def modmm(x, sc, sh, w, tm, tn, out_dtype=F32):
    b, L, k = x.shape
    n = w.shape[1]
    return pl.pallas_call(
        _modmm_kernel,
        grid=(b, L // tm, n // tn),
        in_specs=[pl.BlockSpec((None, tm, k), lambda bi, i, j: (bi, i, 0)),
                  pl.BlockSpec((None, 1, k), lambda bi, i, j: (bi, 0, 0)),
                  pl.BlockSpec((None, 1, k), lambda bi, i, j: (bi, 0, 0)),
                  pl.BlockSpec((k, tn), lambda bi, i, j: (0, j))],
        out_specs=pl.BlockSpec((None, tm, tn), lambda bi, i, j: (bi, i, j)),
        out_shape=jax.ShapeDtypeStruct((b, L, n), out_dtype),
        scratch_shapes=[pltpu.VMEM((tm, k), BF16)],
        compiler_params=_cparams(("parallel", "parallel", "arbitrary")),
        name="modmm",
    )(x, sc, sh, w)


def _conv_silu_kernel(x_ref, w_ref, b_ref, o_ref):
    x = x_ref[...]
    L = x.shape[0]
    row = lax.broadcasted_iota(jnp.int32, x.shape, 0)
    prev = jnp.where(row == 0, 0.0, pltpu.roll(x, 1, axis=0))
    nxt = jnp.where(row == L - 1, 0.0, pltpu.roll(x, L - 1, axis=0))
    w = w_ref[...]
    y = prev * w[0:1] + x * w[1:2] + nxt * w[2:3] + b_ref[...]
    o_ref[...] = _silu(y)


def conv_silu(pr, conv_w, conv_b, tc=512):
    b, L, _ = pr.shape
    off = PR_XBC // tc
    return pl.pallas_call(
        _conv_silu_kernel,
        grid=(b, SSD_CONV_DIM // tc),
        in_specs=[pl.BlockSpec((None, L, tc), lambda bi, j: (bi, 0, off + j)),
                  pl.BlockSpec((3, tc), lambda bi, j: (0, j)),
                  pl.BlockSpec((1, tc), lambda bi, j: (0, j))],
        out_specs=pl.BlockSpec((None, L, tc), lambda bi, j: (bi, 0, j)),
        out_shape=jax.ShapeDtypeStruct((b, L, SSD_CONV_DIM), F32),
        compiler_params=_cparams(("parallel", "parallel")),
        name="conv_silu",
    )(pr, conv_w, conv_b)


def _ssd_kernel(x_ref, bc_ref, dtr_ref, dtb_ref, alog_ref, s0_ref, y_ref, sfin_ref, st_ref):
    Q = SSD_Q
    d = pl.program_id(0)
    c = pl.program_id(2)
    nc = pl.num_programs(2)
    fwd = d == 0

    @pl.when(c == 0)
    def _():
        st_ref[...] = s0_ref[...]

    row = lax.broadcasted_iota(jnp.int32, (Q, Q), 0)
    col = lax.broadcasted_iota(jnp.int32, (Q, Q), 1)
    sgn = 1 - 2 * d
    incl = (row - col) * sgn >= 0
    tri = jnp.where(incl, 1.0, 0.0).astype(BF16)

    dt_all = _softplus(dtr_ref[:, 0:LANES] + dtb_ref[...])
    dA_all = dt_all * (-jnp.exp(alog_ref[...]))
    acum_all = _dot_exact_lhs(tri, dA_all)
    acum_t = acum_all.T

    lane_h = lax.broadcasted_iota(jnp.int32, (LANES, SSD_HPG * SSD_HEADDIM), 0)
    col_h = lax.broadcasted_iota(jnp.int32, (LANES, SSD_HPG * SSD_HEADDIM), 1) // SSD_HEADDIM
    lane128 = lax.broadcasted_iota(jnp.int32, (Q, LANES), 1)
    m_lo = lane128 < SSD_HEADDIM

    for g in range(SSD_GROUPS):
        expand = jnp.where(lane_h == d * SSD_HEADS + g * SSD_HPG + col_h, 1.0, 0.0).astype(BF16)
        dt_e = _dot_exact_rhs(dt_all, expand)
        a_e = _dot_exact_rhs(acum_all, expand)
        a_tot = jnp.where(fwd, a_e[Q - 1:Q], a_e[0:1])
        xg = x_ref[:, g * 512:(g + 1) * 512]
        xdt = xg * dt_e
        bg = bc_ref[:, g * SSD_STATE:(g + 1) * SSD_STATE]
        cg = bc_ref[:, SSD_GROUPS * SSD_STATE + g * SSD_STATE:SSD_GROUPS * SSD_STATE + (g + 1) * SSD_STATE]
        bgb = bg.astype(BF16)
        cgb = cg.astype(BF16)
        scores = _dot_nt(cgb, bgb)
        st = st_ref[g]
        y_off = _dot(cgb, st.astype(BF16)) * jnp.exp(a_e)
        xdt_b = xdt.astype(BF16)
        for pr_i in range(SSD_HPG // 2):
            xp = xdt_b[:, pr_i * LANES:(pr_i + 1) * LANES]
            acc = y_off[:, pr_i * LANES:(pr_i + 1) * LANES]
            for half in range(2):
                r = 2 * pr_i + half
                hrow = g * SSD_HPG + r
                a_col = jnp.where(fwd, acum_all[:, hrow:hrow + 1],
                                  acum_all[:, SSD_HEADS + hrow:SSD_HEADS + hrow + 1])
                a_row = jnp.where(fwd, acum_t[hrow:hrow + 1, :],
                                  acum_t[SSD_HEADS + hrow:SSD_HEADS + hrow + 1, :])
                seg = jnp.minimum(a_col - a_row, 0.0)
                m = jnp.where(incl, scores * jnp.exp(seg), 0.0).astype(BF16)
                xh = jnp.where(m_lo if half == 0 else jnp.logical_not(m_lo), xp, jnp.zeros_like(xp))
                acc = acc + _dot(m, xh)
            y_ref[:, g * 512 + pr_i * LANES:g * 512 + (pr_i + 1) * LANES] = acc
        xde = (xdt * jnp.exp(a_tot - a_e)).astype(BF16)
        st_ref[g] = st * jnp.exp(a_tot) + _dot_tn(bgb, xde)

    @pl.when(c == nc - 1)
    def _():
        sfin_ref[...] = st_ref[...]


def ssd_scan(xbc, pr, dt_bias128, a_log128, s0):
    b, L, _ = xbc.shape
    Q = SSD_Q
    nc = L // Q

    def cidx(d, c):
        return jnp.where(d == 0, c, nc - 1 - c)

    return pl.pallas_call(
        _ssd_kernel,
        grid=(2, b, nc),
        in_specs=[pl.BlockSpec((None, Q, SSD_WIDTH), lambda d, bi, c: (bi, cidx(d, c), 0)),
                  pl.BlockSpec((None, Q, 512), lambda d, bi, c: (bi, cidx(d, c), SSD_WIDTH // 512)),
                  pl.BlockSpec((None, Q, 256), lambda d, bi, c: (bi, cidx(d, c), PR_DT // 256)),
                  pl.BlockSpec((1, LANES), lambda d, bi, c: (0, 0)),
                  pl.BlockSpec((1, LANES), lambda d, bi, c: (0, 0)),
                  pl.BlockSpec((None, None, SSD_GROUPS, SSD_STATE, 512), lambda d, bi, c: (d, bi, 0, 0, 0))],
        out_specs=[pl.BlockSpec((None, None, Q, SSD_WIDTH), lambda d, bi, c: (d, bi, cidx(d, c), 0)),
                   pl.BlockSpec((None, None, SSD_GROUPS, SSD_STATE, 512), lambda d, bi, c: (d, bi, 0, 0, 0))],
        out_shape=[jax.ShapeDtypeStruct((2, b, L, SSD_WIDTH), F32),
                   jax.ShapeDtypeStruct((2, b, SSD_GROUPS, SSD_STATE, 512), F32)],
        scratch_shapes=[pltpu.VMEM((SSD_GROUPS, SSD_STATE, 512), F32)],
        compiler_params=_cparams(("parallel", "parallel", "arbitrary")),
        name="ssd_scan",
    )(xbc, xbc, pr, dt_bias128, a_log128, s0)


def _fnet_chan_kernel(f_ref, cs_ref, o_ref):
    res = _dot(f_ref[...].astype(BF16), cs_ref[...])
    o_ref[0] = res[:, :FNET_GW].astype(o_ref.dtype)
    o_ref[1] = res[:, FNET_GW:].astype(o_ref.dtype)


def fnet_chan(pr, cs, tm):
    b, L, _ = pr.shape
    off = PR_F // FNET_GW
    return pl.pallas_call(
        _fnet_chan_kernel,
        grid=(b, L // tm, FNET_GROUPS),
        in_specs=[pl.BlockSpec((None, tm, FNET_GW), lambda bi, i, g: (bi, i, off + g)),
                  pl.BlockSpec((FNET_GW, 2 * FNET_GW), lambda bi, i, g: (0, 0))],
        out_specs=pl.BlockSpec((None, 2, tm, FNET_GW), lambda bi, i, g: (bi, 0, i, g)),
        out_shape=jax.ShapeDtypeStruct((b, 2, L, FNET_WIDTH), BF16),
        compiler_params=_cparams(("parallel", "parallel", "parallel")),
        name="fnet_chan",
    )(pr, cs)


def _fnet_pos_kernel(w_ref, x_ref, o_ref):
    o_ref[...] = _dot(w_ref[...], x_ref[...])


def fnet_pos(wpos, xcs, tm, tn):
    b, k, n = xcs.shape
    L = wpos.shape[0]
    return pl.pallas_call(
        _fnet_pos_kernel,
        grid=(b, n // tn, L // tm),
        in_specs=[pl.BlockSpec((tm, k), lambda bi, j, i: (i, 0)),
                  pl.BlockSpec((None, k, tn), lambda bi, j, i: (bi, 0, j))],
        out_specs=pl.BlockSpec((None, tm, tn), lambda bi, j, i: (bi, i, j)),
        out_shape=jax.ShapeDtypeStruct((b, L, n), F32),
        compiler_params=_cparams(("parallel", "parallel", "parallel")),
        name="fnet_pos",
    )(wpos, xcs)


def _dft_tables(n):
    k = np.arange(n, dtype=np.int64)
    ang = 2.0 * np.pi * ((k[:, None] * k[None, :]) % n).astype(np.float64) / n
    s = 1.0 / math.sqrt(n)
    return np.cos(ang) * s, np.sin(ang) * s


def _even_out_kernel(y0_ref, y1_ref, xs_ref, z_ref, f_ref, dsk_ref, nw_ref, w_ref, xres_ref, gate_ref,
                     lng_ref, lnb_ref, sc_ref, sh_ref, xo_ref, h2_ref, lhs_ref):
    y = y0_ref[...] + y1_ref[...] + dsk_ref[...] * xs_ref[...]
    u = y * _silu(z_ref[...])
    gw = SSD_WIDTH // SSD_GROUPS
    for g in range(SSD_GROUPS):
        ug = u[:, g * gw:(g + 1) * gw]
        ms = jnp.mean(ug * ug, axis=-1, keepdims=True)
        lhs_ref[:, g * gw:(g + 1) * gw] = (ug * lax.rsqrt(ms + LN_EPS) * nw_ref[:, g * gw:(g + 1) * gw]).astype(BF16)
    lhs_ref[:, SSD_WIDTH:] = f_ref[...].astype(BF16)
    ymix = _dot(lhs_ref[...], w_ref[...])
    xn = _layer_norm(ALPHA * xres_ref[...] + gate_ref[...] * ymix, lng_ref[...], lnb_ref[...])
    xo_ref[...] = xn
    h2_ref[...] = xn * (1.0 + sc_ref[...]) + sh_ref[...]


def even_out(y2, xbc, pr, fmix, dskip, norm_w, w_out, x_res, gate, ln_g, ln_b, sc2, sh2, tm):
    b, L, D = x_res.shape
    vec = lambda: pl.BlockSpec((1, D), lambda bi, i: (0, 0))
    bvec = lambda: pl.BlockSpec((None, 1, D), lambda bi, i: (bi, 0, 0))
    return pl.pallas_call(
        _even_out_kernel,
        grid=(b, L // tm),
        in_specs=[pl.BlockSpec((None, None, tm, SSD_WIDTH), lambda bi, i: (0, bi, i, 0)),
                  pl.BlockSpec((None, None, tm, SSD_WIDTH), lambda bi, i: (1, bi, i, 0)),
                  pl.BlockSpec((None, tm, SSD_WIDTH), lambda bi, i: (bi, i, 0)),
                  pl.BlockSpec((None, tm, SSD_WIDTH), lambda bi, i: (bi, i, PR_Z // SSD_WIDTH)),
                  pl.BlockSpec((None, tm, FNET_WIDTH), lambda bi, i: (bi, i, 0)),
                  pl.BlockSpec((1, SSD_WIDTH), lambda bi, i: (0, 0)),
                  pl.BlockSpec((1, SSD_WIDTH), lambda bi, i: (0, 0)),
                  pl.BlockSpec((D, D), lambda bi, i: (0, 0)),
                  pl.BlockSpec((None, tm, D), lambda bi, i: (bi, i, 0)),
                  bvec(), vec(), vec(), bvec(), bvec()],
        out_specs=[pl.BlockSpec((None, tm, D), lambda bi, i: (bi, i, 0)),
                   pl.BlockSpec((None, tm, D), lambda bi, i: (bi, i, 0))],
        out_shape=[jax.ShapeDtypeStruct((b, L, D), F32), jax.ShapeDtypeStruct((b, L, D), F32)],
        scratch_shapes=[pltpu.VMEM((tm, D), BF16)],
        compiler_params=_cparams(("parallel", "parallel")),
        name="even_out",
    )(y2, y2, xbc, pr, fmix, dskip, norm_w, w_out, x_res, gate, ln_g, ln_b, sc2, sh2)


def _router_kernel(h_ref, w_ref, bias_ref, o_ref):
    logits = _dot_x3(h_ref[...], w_ref[...])
    s = jax.nn.sigmoid(logits)
    lane = lax.broadcasted_iota(jnp.int32, s.shape, 1)
    neg = jnp.float32(-jnp.inf)
    ssel = s + bias_ref[...]
    big = jnp.int32(4 * LANES)

    def top2(mask):
        v = jnp.where(mask, ssel, neg)
        m1 = jnp.max(v, axis=-1, keepdims=True)
        i1 = jnp.min(jnp.where(v == m1, lane, big), axis=-1, keepdims=True)
        v2 = jnp.where(lane == i1, neg, v)
        m2 = jnp.max(v2, axis=-1, keepdims=True)
        i2 = jnp.min(jnp.where(v2 == m2, lane, big), axis=-1, keepdims=True)
        return m1 + m2, i1, i2

    best, bi1, bi2 = top2(lane < EXPERTS_PER_GROUP)
    for g in range(1, N_GROUPS):
        sc, i1, i2 = top2((lane >= g * EXPERTS_PER_GROUP) & (lane < (g + 1) * EXPERTS_PER_GROUP))
        better = sc > best
        best = jnp.where(better, sc, best)
        bi1 = jnp.where(better, i1, bi1)
        bi2 = jnp.where(better, i2, bi2)
    s1 = jnp.sum(jnp.where(lane == bi1, s, 0.0), axis=-1, keepdims=True)
    s2 = jnp.sum(jnp.where(lane == bi2, s, 0.0), axis=-1, keepdims=True)
    tot = s1 + s2
    out = jnp.where(lane == 0, bi1.astype(F32),
                    jnp.where(lane == 1, bi2.astype(F32),
                              jnp.where(lane == 2, s1 / tot, jnp.where(lane == 3, s2 / tot, 0.0))))
    o_ref[...] = out


def router(h, w128, bias128, tm):
    T, D = h.shape
    return pl.pallas_call(
        _router_kernel,
        grid=(T // tm,),
        in_specs=[pl.BlockSpec((tm, D), lambda i: (i, 0)),
                  pl.BlockSpec((D, LANES), lambda i: (0, 0)),
                  pl.BlockSpec((1, LANES), lambda i: (0, 0))],
        out_specs=pl.BlockSpec((tm, LANES), lambda i: (i, 0)),
        out_shape=jax.ShapeDtypeStruct((T, LANES), F32),
        compiler_params=_cparams(("parallel",)),
        name="router",
    )(h, w128, bias128)


def _moe_kernel(be_ref, nused_ref, src_ref, nxt_ref, h_hbm, wg_ref, wu_ref, wd_ref, o_ref,
                xb0, xb1, wgb, wub, wdb, sem):
    i = pl.program_id(0)
    nu = nused_ref[0]
    bm = xb0.shape[0]

    def issue(idx_ref, dst, s):
        for r in range(bm):
            pltpu.make_async_copy(h_hbm.at[pl.ds(idx_ref[0, 0, r], 1)], dst.at[pl.ds(r, 1)], sem.at[s]).start()

    def wait(dst, s):
        pltpu.make_async_copy(h_hbm.at[pl.ds(0, bm)], dst, sem.at[s]).wait()

    @pl.when(i == 0)
    def _():
        issue(src_ref, xb0, 0)

    @pl.when(i < nu)
    def _():
        prev = be_ref[jnp.maximum(i - 1, 0)]

        @pl.when((i == 0) | (be_ref[i] != prev))
        def _():
            wgb[...] = wg_ref[...].astype(BF16)
            wub[...] = wu_ref[...].astype(BF16)
            wdb[...] = wd_ref[...].astype(BF16)

        for par in range(2):
            cur, nxt = (xb0, xb1) if par == 0 else (xb1, xb0)

            @pl.when(i % 2 == par)
            def _(cur=cur, nxt=nxt, par=par):
                wait(cur, par)
                issue(nxt_ref, nxt, 1 - par)
                x = cur[...].astype(BF16)
                hid = _silu(_dot(x, wgb[...])) * _dot(x, wub[...])
                o_ref[...] = _dot(hid.astype(BF16), wdb[...])

                @pl.when(i == nu - 1)
                def _():
                    wait(nxt, 1 - par)

    @pl.when(i >= nu)
    def _():
        o_ref[...] = jnp.zeros_like(o_ref)


def moe_experts(h, block_e, nused, src_tok, w_gate, w_up, w_down):
    T, D = h.shape
    nblk = block_e.shape[0]
    bm = MOE_BM
    last = lambda i, nu: jnp.minimum(i, nu[0] - 1)
    gs = pltpu.PrefetchScalarGridSpec(
        num_scalar_prefetch=2,
        grid=(nblk,),
        in_specs=[pl.BlockSpec((1, 1, bm), lambda i, be, nu: (last(i, nu), 0, 0), memory_space=pltpu.SMEM),
                  pl.BlockSpec((1, 1, bm), lambda i, be, nu: (last(i + 1, nu), 0, 0), memory_space=pltpu.SMEM),
                  pl.BlockSpec(memory_space=pl.ANY),
                  pl.BlockSpec((None, D, D_EXPERT), lambda i, be, nu: (be[last(i, nu)], 0, 0)),
                  pl.BlockSpec((None, D, D_EXPERT), lambda i, be, nu: (be[last(i, nu)], 0, 0)),
                  pl.BlockSpec((None, D_EXPERT, D), lambda i, be, nu: (be[last(i, nu)], 0, 0))],
        out_specs=pl.BlockSpec((bm, D), lambda i, be, nu: (i, 0)),
        scratch_shapes=[pltpu.VMEM((bm, D), F32),
                        pltpu.VMEM((bm, D), F32),
                        pltpu.VMEM((D, D_EXPERT), BF16),
                        pltpu.VMEM((D, D_EXPERT), BF16),
                        pltpu.VMEM((D_EXPERT, D), BF16),
                        pltpu.SemaphoreType.DMA((2,))],
    )
    src3 = src_tok.reshape(nblk, 1, bm)
    return pl.pallas_call(
        _moe_kernel,
        grid_spec=gs,
        out_shape=jax.ShapeDtypeStruct((nblk * bm, D), F32),
        compiler_params=_cparams(("arbitrary",)),
        name="moe_experts",
    )(block_e, nused, src3, src3, h, w_gate, w_up, w_down)


def _moe_combine_kernel(dest_ref, nxt_ref, yb_hbm, rw_ref, x_ref, gate_ref, lng_ref, lnb_ref, o_ref, yb0, yb1, sem):
    i = pl.program_id(0)
    n = pl.num_programs(0)
    tm = x_ref.shape[0]

    def issue(idx_ref, dst, s):
        for r in range(tm):
            pltpu.make_async_copy(yb_hbm.at[pl.ds(idx_ref[0, 0, r], 1)], dst.at[0, pl.ds(r, 1)], sem.at[s]).start()
            pltpu.make_async_copy(yb_hbm.at[pl.ds(idx_ref[0, 0, tm + r], 1)], dst.at[1, pl.ds(r, 1)],
                                  sem.at[s]).start()

    def wait(dst, s):
        pltpu.make_async_copy(yb_hbm.at[pl.ds(0, tm)], dst.at[0], sem.at[s]).wait()
        pltpu.make_async_copy(yb_hbm.at[pl.ds(0, tm)], dst.at[1], sem.at[s]).wait()

    @pl.when(i == 0)
    def _():
        issue(dest_ref, yb0, 0)

    for par in range(2):
        cur, nxt = (yb0, yb1) if par == 0 else (yb1, yb0)

        @pl.when(i % 2 == par)
        def _(cur=cur, nxt=nxt, par=par):
            wait(cur, par)
            issue(nxt_ref, nxt, 1 - par)
            rw = rw_ref[...]
            y2 = rw[:, 2:3] * cur[0] + rw[:, 3:4] * cur[1]
            o_ref[...] = _layer_norm(ALPHA * x_ref[...] + gate_ref[...] * y2, lng_ref[...], lnb_ref[...])

            @pl.when(i == n - 1)
            def _():
                wait(nxt, 1 - par)


def moe_combine(dest_tiles, ybuf, rw, x, gates, gate_tile_map, ln_g, ln_b, tm):
    T, D = x.shape
    nt = T // tm
    return pl.pallas_call(
        _moe_combine_kernel,
        grid=(nt,),
        in_specs=[pl.BlockSpec((1, 1, 2 * tm), lambda i: (i, 0, 0), memory_space=pltpu.SMEM),
                  pl.BlockSpec((1, 1, 2 * tm), lambda i: (jnp.minimum(i + 1, nt - 1), 0, 0), memory_space=pltpu.SMEM),
                  pl.BlockSpec(memory_space=pl.ANY),
                  pl.BlockSpec((tm, LANES), lambda i: (i, 0)),
                  pl.BlockSpec((tm, D), lambda i: (i, 0)),
                  pl.BlockSpec((None, 1, D), lambda i: (gate_tile_map(i), 0, 0)),
                  pl.BlockSpec((1, D), lambda i: (0, 0)),
                  pl.BlockSpec((1, D), lambda i: (0, 0))],
        out_specs=pl.BlockSpec((tm, D), lambda i: (i, 0)),
        out_shape=jax.ShapeDtypeStruct((T, D), F32),
        scratch_shapes=[pltpu.VMEM((2, tm, D), F32), pltpu.VMEM((2, tm, D), F32), pltpu.SemaphoreType.DMA((2,))],
        compiler_params=_cparams(("arbitrary",)),
        name="moe_combine",
    )(dest_tiles, dest_tiles, ybuf, rw, x, gates, ln_g, ln_b)


def moe_layer(h2, x_res, gates, gate_tile_map, router_w128, router_b128, w_gate, w_up, w_down, ln_g, ln_b, tm=256):
    T, D = h2.shape
    bm = MOE_BM
    rw = router(h2, router_w128, router_b128, tm)
    idx = rw[:, :TOP_K].astype(jnp.int32)
    e_flat = idx.reshape(-1)
    A = e_flat.shape[0]
    onehot = (e_flat[:, None] == jnp.arange(N_EXPERTS, dtype=jnp.int32)[None, :]).astype(jnp.int32)
    csum = jnp.cumsum(onehot, axis=0)
    counts = csum[-1]
    rank = jnp.sum(onehot * csum, axis=1) - 1
    padded = (counts + bm - 1) // bm * bm
    ends = jnp.cumsum(padded)
    pstart = ends - padded
    dest = pstart[e_flat] + rank
    nblk = (A + N_EXPERTS * (bm - 1) + bm - 1) // bm
    P = nblk * bm
    tok = jnp.arange(A, dtype=jnp.int32) // TOP_K
    src_tok = jnp.zeros((P,), jnp.int32).at[dest].set(tok)
    blk_start = jnp.arange(nblk, dtype=jnp.int32) * bm
    block_e = jnp.minimum(jnp.sum((ends[None, :] <= blk_start[:, None]).astype(jnp.int32), axis=1),
                          N_EXPERTS - 1).astype(jnp.int32)
    nused = (ends[-1] // bm).astype(jnp.int32).reshape(1)
    ybuf = moe_experts(h2, block_e, nused, src_tok, w_gate, w_up, w_down)
    dest2 = dest.reshape(T // tm, tm, TOP_K)
    dest_tiles = jnp.concatenate([dest2[:, :, 0], dest2[:, :, 1]], axis=1).reshape(T // tm, 1, 2 * tm)
    return moe_combine(dest_tiles.astype(jnp.int32), ybuf, rw, x_res, gates, gate_tile_map, ln_g, ln_b, tm)


def _rwkv_proj_kernel(x_ref, xp_ref, xn_ref, sc_ref, sh_ref, mu_ref, w_ref, o_ref, h_ref, xx_ref, mix_ref,
                      *, grid_w, n_main):
    i = pl.program_id(1)
    j = pl.program_id(2)
    nt = pl.num_programs(1)
    tm, D = h_ref.shape
    q = D // 4

    @pl.when(j == 0)
    def _():
        sc = 1.0 + sc_ref[...]
        sh = sh_ref[...]
        h = x_ref[...] * sc + sh
        h_ref[...] = h
        row = lax.broadcasted_iota(jnp.int32, (tm, q), 0)
        if grid_w is None:
            prev = lambda a: jnp.where(row == 0, 0.0, pltpu.roll(a, 1, axis=0))
            nxt = lambda a: jnp.where(row == tm - 1, 0.0, pltpu.roll(a, tm - 1, axis=0))
            parts = [prev(h[:, 0:q]), nxt(h[:, q:2 * q]), prev(h[:, 2 * q:3 * q]), nxt(h[:, 3 * q:])]
        else:
            wpos = row % grid_w
            left = jnp.where(wpos == 0, 0.0, pltpu.roll(h[:, 0:q], 1, axis=0))
            right = jnp.where(wpos == grid_w - 1, 0.0, pltpu.roll(h[:, q:2 * q], tm - 1, axis=0))
            hp = xp_ref[:, 2 * q:3 * q] * sc[:, 2 * q:3 * q] + sh[:, 2 * q:3 * q]
            hn = xn_ref[:, 3 * q:] * sc[:, 3 * q:] + sh[:, 3 * q:]
            hp = jnp.where(i == 0, 0.0, hp)
            hn = jnp.where(i == nt - 1, 0.0, hn)
            up = jnp.concatenate([hp, h[:tm - grid_w, 2 * q:3 * q]], axis=0)
            down = jnp.concatenate([h[grid_w:, 3 * q:], hn], axis=0)
            parts = [left, right, up, down]
        for k in range(4):
            xx_ref[:, k * q:(k + 1) * q] = parts[k] - h[:, k * q:(k + 1) * q]

    is_main = j < 3 * n_main
    new_stream = jnp.where(is_main, j % n_main == 0, True)

    @pl.when(new_stream)
    def _():
        mix_ref[...] = (h_ref[...] + xx_ref[...] * mu_ref[...]).astype(BF16)

    o_ref[...] = _dot(mix_ref[...], w_ref[...]).astype(o_ref.dtype)


def rwkv_proj(x, sc, sh, mu6, wcat, tm, tn, grid_w):
    b, L, D = x.shape
    n = wcat.shape[1]
    n_main = D // tn
    nj = n // tn
    gw = 64 if grid_w is None else grid_w
    nh = L // gw

    def stream(j):
        return jnp.where(j < 3 * n_main, j // n_main, 3 + (j - 3 * n_main) // (LORA_PAD // tn))

    kern = functools.partial(_rwkv_proj_kernel, grid_w=grid_w, n_main=n_main)
    return pl.pallas_call(
        kern,
        grid=(b, L // tm, nj),
        in_specs=[pl.BlockSpec((None, tm, D), lambda bi, i, j: (bi, i, 0)),
                  pl.BlockSpec((None, gw, D), lambda bi, i, j: (bi, jnp.maximum(i * (tm // gw) - 1, 0), 0)),
                  pl.BlockSpec((None, gw, D), lambda bi, i, j: (bi, jnp.minimum((i + 1) * (tm // gw), nh - 1), 0)),
                  pl.BlockSpec((None, 1, D), lambda bi, i, j: (bi, 0, 0)),
                  pl.BlockSpec((None, 1, D), lambda bi, i, j: (bi, 0, 0)),
                  pl.BlockSpec((None, 1, D), lambda bi, i, j: (stream(j), 0, 0)),
                  pl.BlockSpec((D, tn), lambda bi, i, j: (0, j))],
        out_specs=pl.BlockSpec((None, tm, tn), lambda bi, i, j: (bi, i, j)),
        out_shape=jax.ShapeDtypeStruct((b, L, n), BF16),
        scratch_shapes=[pltpu.VMEM((tm, D), F32), pltpu.VMEM((tm, D), F32), pltpu.VMEM((tm, D), BF16)],
        compiler_params=_cparams(("parallel", "parallel", "arbitrary")),
        name="rwkv_proj",
    )(x, x, x, sc, sh, mu6, wcat)


def _seg_sum64(x, ones_blk):
    hi, lo = _split2(x)
    return _dot(hi, ones_blk) + _dot(lo, ones_blk)


def _rwkv_post_kernel(r_ref, k_ref, v_ref, hw_ref, ha_ref, hg_ref, w2_ref, a2_ref, g2_ref,
                      w0_ref, a0_ref, kk_ref, ka_ref, rk_ref,
                      ro_ref, kko_ref, vo_ref, bo_ref, go_ref, lw_ref, kd_ref, kao_ref):
    r = r_ref[...].astype(F32)
    k = k_ref[...].astype(F32)
    v = v_ref[...].astype(F32)
    tm, D = r.shape
    li = lax.broadcasted_iota(jnp.int32, (LANES, LANES), 0) // RWKV_HEADSIZE
    lj = lax.broadcasted_iota(jnp.int32, (LANES, LANES), 1) // RWKV_HEADSIZE
    ones_blk = jnp.where(li == lj, 1.0, 0.0).astype(BF16)
    g = _dot(jax.nn.sigmoid(hg_ref[...].astype(F32)).astype(BF16), g2_ref[...])
    thw = jnp.tanh(hw_ref[...].astype(F32)).astype(BF16)
    ha = ha_ref[...]
    kkr = k * kk_ref[...]
    rkr = r * k * rk_ref[...]
    aas = []
    for e in range(2):
        dw = _dot(thw, w2_ref[e])
        wl = -_softplus(-(w0_ref[e] + dw)) - 0.5
        lw = -jnp.exp(wl)
        a = jax.nn.sigmoid(a0_ref[e] + _dot(ha, a2_ref[e]))
        aas.append(a)
        kd = k * (1.0 + (a - 1.0) * ka_ref[...])
        for p in range(RWKV_PAIRS):
            sl = slice(p * LANES, (p + 1) * LANES)
            lw_ref[e, p] = lw[:, sl]
            kd_ref[e, p] = kd[:, sl].astype(kd_ref.dtype)
    for p in range(RWKV_PAIRS):
        sl = slice(p * LANES, (p + 1) * LANES)
        kp = kkr[:, sl]
        nrm = jnp.maximum(jnp.sqrt(_seg_sum64(kp * kp, ones_blk)), 1e-12)
        kkp = kp / nrm
        kko_ref[p] = kkp.astype(kko_ref.dtype)
        for e in range(2):
            kao_ref[e, p] = (kkp * aas[e][:, sl]).astype(kao_ref.dtype)
        bo_ref[p] = (_seg_sum64(rkr[:, sl], ones_blk) * v[:, sl]).astype(bo_ref.dtype)
        ro_ref[p] = r[:, sl].astype(ro_ref.dtype)
        vo_ref[p] = v[:, sl].astype(vo_ref.dtype)
        go_ref[p] = g[:, sl].astype(go_ref.dtype)


def rwkv_post(proj, w2z, a2z, g2, w0, a0, k_k, k_a, r_k, tm):
    b, L, _ = proj.shape
    D = D_MODEL
    NP = RWKV_PAIRS
    col = lambda off, w: pl.BlockSpec((None, tm, w), lambda bi, i: (bi, i, off // w))
    full = lambda shape: pl.BlockSpec(shape, lambda bi, i: (0,) * len(shape))
    pair = lambda: pl.BlockSpec((None, NP, tm, LANES), lambda bi, i: (bi, 0, i, 0))
    pair2 = lambda: pl.BlockSpec((2, None, NP, tm, LANES), lambda bi, i: (0, bi, 0, i, 0))
    sh1 = lambda dt: jax.ShapeDtypeStruct((b, NP, L, LANES), dt)
    sh2 = lambda dt: jax.ShapeDtypeStruct((2, b, NP, L, LANES), dt)
    return pl.pallas_call(
        _rwkv_post_kernel,
        grid=(b, L // tm),
        in_specs=[col(0, D), col(D, D), col(2 * D, D),
                  col(3 * D, LORA_PAD), col(3 * D + LORA_PAD, LORA_PAD), col(3 * D + 2 * LORA_PAD, LORA_PAD),
                  full((2, LORA_PAD, D)), full((2, LORA_PAD, D)), full((LORA_PAD, D)),
                  full((2, 1, D)), full((2, 1, D)), full((1, D)), full((1, D)), full((1, D))],
        out_specs=[pair(), pair(), pair(), pair(), pair(), pair2(), pair2(), pair2()],
        out_shape=[sh1(BF16), sh1(BF16), sh1(BF16), sh1(BF16), sh1(BF16), sh2(F32), sh2(BF16), sh2(BF16)],
        compiler_params=_cparams(("parallel", "parallel")),
        name="rwkv_post",
    )(proj, proj, proj, proj, proj, proj, w2z, a2z, g2, w0, a0, k_k, k_a, r_k)


def _wkv_kernel(r_ref, kk_ref, v_ref, lw_ref, kd_ref, ka_ref, s0_ref, y_ref, sfin_ref, st_ref):
    T = WKV_T
    d = pl.program_id(0)
    c = pl.program_id(2)
    nc = pl.num_programs(2)
    fwd = d == 0

    @pl.when(c == 0)
    def _():
        st_ref[...] = s0_ref[...]

    row = lax.broadcasted_iota(jnp.int32, (T, T), 0)
    col = lax.broadcasted_iota(jnp.int32, (T, T), 1)
    sgn = 1 - 2 * d
    tri = jnp.where((row - col) * sgn >= 0, 1.0, 0.0).astype(BF16)
    row2 = lax.broadcasted_iota(jnp.int32, (T, 2 * T), 0)
    col2 = lax.broadcasted_iota(jnp.int32, (T, 2 * T), 1) % T
    incl2 = (row2 - col2) * sgn >= 0
    strict2 = (row2 - col2) * sgn > 0
    lane = lax.broadcasted_iota(jnp.int32, (T, LANES), 1)
    m_a = lane < RWKV_HEADSIZE
    bi = lax.broadcasted_iota(jnp.int32, (LANES, LANES), 0) // RWKV_HEADSIZE
    bj = lax.broadcasted_iota(jnp.int32, (LANES, LANES), 1) // RWKV_HEADSIZE
    blockdiag = bi == bj
    eye2 = jnp.where(row2 == col2, 1.0, 0.0)
    blk = []
    size = 8
    while size <= T:
        blk.append(row2 // size == col2 // size)
        size *= 2

    def stack2(z):
        zero = jnp.zeros_like(z)
        return jnp.concatenate([jnp.where(m_a, z, zero), jnp.where(m_a, zero, z)], axis=0)

    def cast(z):
        return z.astype(BF16)

    def pairmul(xb, yb):
        return _dot(xb, stack2(yb))

    def each(f, *ls):
        return [f(*a) for a in zip(*ls)]

    ps = list(range(RWKV_PAIRS))
    lw = [lw_ref[p] for p in ps]
    hi = each(cast, lw)
    mid = each(lambda a, h_: cast(a - h_.astype(F32)), lw, hi)
    cum = each(lambda h_, m_: _dot(tri, h_) + _dot(tri, m_), hi, mid)
    e_neg = each(lambda cm: jnp.exp(-cm), cum)
    at = each(lambda p, cm, l_: -kk_ref[p].astype(F32) * jnp.exp(cm - l_), ps, cum, lw)
    rt = each(lambda p, cm: r_ref[p].astype(F32) * jnp.exp(cm), ps, cum)
    bt = each(lambda p, en: cast(ka_ref[p].astype(F32) * en), ps, e_neg)
    kt = each(lambda p, en: cast(kd_ref[p].astype(F32) * en), ps, e_neg)
    x1 = each(lambda a, r_: cast(jnp.concatenate([a, r_], axis=0)), at, rt)
    x2s = each(lambda b_, k_: jnp.concatenate([stack2(b_), stack2(k_)], axis=0), bt, kt)
    gall = each(_dot_nt, x1, x2s)
    lc = each(lambda g_: jnp.where(strict2, g_[:T, :2 * T], 0.0), gall)
    n1 = each(lambda l_: cast(jnp.where(blk[0], l_, 0.0)), lc)
    n2 = each(lambda a: cast(pairmul(a, a)), n1)
    n4 = each(lambda a: cast(pairmul(a, a)), n2)
    inv = each(lambda a: eye2 + a.astype(F32), n1)
    inv = each(lambda iv, a: iv + pairmul(cast(iv), a), inv, n2)
    inv = each(lambda iv, a: iv + pairmul(cast(iv), a), inv, n4)
    for lvl in range(1, len(blk)):
        offm = blk[lvl] & jnp.logical_not(blk[lvl - 1])
        ivb = each(cast, inv)
        t1 = each(lambda l_, ib: cast(pairmul(cast(jnp.where(offm, l_, 0.0)), ib)), lc, ivb)
        inv = each(lambda iv, ib, t_: iv + pairmul(ib, t_), inv, ivb, t1)
    invb = each(cast, inv)
    lak = each(lambda g_: cast(jnp.where(strict2, g_[:T, 2 * T:], 0.0)), gall)
    rbk = each(lambda g_: cast(jnp.concatenate([jnp.where(incl2, g_[T:, :2 * T], 0.0),
                                                jnp.where(incl2, g_[T:, 2 * T:], 0.0)], axis=1)), gall)
    v = [v_ref[p] for p in ps]
    v2 = each(stack2, v)
    s = [st_ref[p] for p in ps]
    h = each(lambda x_, s_: _dot_nt(x_, cast(s_)), x1, s)
    wv = each(lambda h_, l_, v_: h_[:T] + _dot(l_, v_), h, lak, v2)
    ub = each(lambda ib, w_: cast(pairmul(ib, cast(w_))), invb, wv)
    y = each(lambda h_, rb_, u_, v_: h_[T:] + _dot(rb_, jnp.concatenate([stack2(u_), v_], axis=0)), h, rbk, ub, v2)
    ds = each(lambda u_, v_, b_, k_: _dot_tn(jnp.concatenate([u_, v_], axis=0), jnp.concatenate([b_, k_], axis=0)),
              ub, v, bt, kt)
    for p in ps:
        y_ref[p] = y[p].astype(y_ref.dtype)
        c_end = jnp.where(fwd, cum[p][T - 1:T], cum[p][0:1])
        st_ref[p] = jnp.where(blockdiag, s[p] + ds[p], 0.0) * jnp.exp(c_end)

    @pl.when(c == nc - 1)
    def _():
        sfin_ref[...] = st_ref[...]


def wkv_scan(r, kk, v, lw, kd, ka, s0):
    b, NP, L, _ = r.shape
    T = WKV_T
    nc = L // T

    def cidx(d, c):
        return jnp.where(d == 0, c, nc - 1 - c)

    shared = lambda: pl.BlockSpec((None, NP, T, LANES), lambda d, bi, c: (bi, 0, cidx(d, c), 0))
    perdir = lambda: pl.BlockSpec((None, None, NP, T, LANES), lambda d, bi, c: (d, bi, 0, cidx(d, c), 0))
    state = lambda: pl.BlockSpec((None, None, NP, LANES, LANES), lambda d, bi, c: (d, bi, 0, 0, 0))
    return pl.pallas_call(
        _wkv_kernel,
        grid=(2, b, nc),
        in_specs=[shared(), shared(), shared(), perdir(), perdir(), perdir(), state()],
        out_specs=[perdir(), state()],
        out_shape=[jax.ShapeDtypeStruct((2, b, NP, L, LANES), F32),
                   jax.ShapeDtypeStruct((2, b, NP, LANES, LANES), F32)],
        scratch_shapes=[pltpu.VMEM((NP, LANES, LANES), F32)],
        compiler_params=_cparams(("parallel", "parallel", "arbitrary")),
        name="wkv_scan",
    )(r, kk, v, lw, kd, ka, s0)


def _odd_out_kernel(y0_ref, y1_ref, bo_ref, g_ref, lw_ref, lb_ref, w_ref, xres_ref, gate_ref,
                    lng_ref, lnb_ref, sc_ref, sh_ref, xo_ref, h2_ref, lhs_ref):
    li = lax.broadcasted_iota(jnp.int32, (LANES, LANES), 0) // RWKV_HEADSIZE
    lj = lax.broadcasted_iota(jnp.int32, (LANES, LANES), 1) // RWKV_HEADSIZE
    ones_blk = jnp.where(li == lj, 1.0, 0.0).astype(BF16)
    inv = 1.0 / RWKV_HEADSIZE
    for p in range(RWKV_PAIRS):
        sl = slice(p * LANES, (p + 1) * LANES)
        y = y0_ref[p] + y1_ref[p]
        mu = _seg_sum64(y, ones_blk) * inv
        dlt = y - mu
        var = _seg_sum64(dlt * dlt, ones_blk) * inv
        yn = dlt * lax.rsqrt(var + GN_EPS) * lw_ref[:, sl] + lb_ref[:, sl]
        lhs_ref[:, sl] = ((yn + bo_ref[p].astype(F32)) * g_ref[p].astype(F32)).astype(BF16)
    ymix = _dot(lhs_ref[...], w_ref[...])
    xn = _layer_norm(ALPHA * xres_ref[...] + gate_ref[...] * ymix, lng_ref[...], lnb_ref[...])
    xo_ref[...] = xn
    h2_ref[...] = xn * (1.0 + sc_ref[...]) + sh_ref[...]


def odd_out(y2, bonus, g, lnx_w, lnx_b, w_o, x_res, gate, ln_g, ln_b, sc2, sh2, tm):
    b, L, D = x_res.shape
    NP = RWKV_PAIRS
    vec = lambda: pl.BlockSpec((1, D), lambda bi, i: (0, 0))
    bvec = lambda: pl.BlockSpec((None, 1, D), lambda bi, i: (bi, 0, 0))
    pair = lambda: pl.BlockSpec((None, NP, tm, LANES), lambda bi, i: (bi, 0, i, 0))
    return pl.pallas_call(
        _odd_out_kernel,
        grid=(b, L // tm),
        in_specs=[pl.BlockSpec((None, None, NP, tm, LANES), lambda bi, i: (0, bi, 0, i, 0)),
                  pl.BlockSpec((None, None, NP, tm, LANES), lambda bi, i: (1, bi, 0, i, 0)),
                  pair(), pair(), vec(), vec(),
                  pl.BlockSpec((D, D), lambda bi, i: (0, 0)),
                  pl.BlockSpec((None, tm, D), lambda bi, i: (bi, i, 0)),
                  bvec(), vec(), vec(), bvec(), bvec()],
        out_specs=[pl.BlockSpec((None, tm, D), lambda bi, i: (bi, i, 0)),
                   pl.BlockSpec((None, tm, D), lambda bi, i: (bi, i, 0))],
        out_shape=[jax.ShapeDtypeStruct((b, L, D), F32), jax.ShapeDtypeStruct((b, L, D), F32)],
        scratch_shapes=[pltpu.VMEM((tm, D), BF16)],
        compiler_params=_cparams(("parallel", "parallel")),
        name="odd_out",
    )(y2, y2, bonus, g, lnx_w, lnx_b, w_o, x_res, gate, ln_g, ln_b, sc2, sh2)


def _pad_cols(w, n):
    return jnp.pad(w, ((0, 0), (0, n - w.shape[1])))


def even_layer(x_lat, x_ctx, ml, mc, p):
    bsz = x_lat.shape[0]
    o1, o2, o3 = SSD_WIDTH, SSD_WIDTH + SSD_CONV_DIM, SSD_WIDTH + SSD_CONV_DIM + 2 * SSD_HEADS
    w_in = p["w_in"]
    w_pad = jnp.concatenate([w_in[:, :o1], w_in[:, o1:o2], w_in[:, o3:],
                             _pad_cols(w_in[:, o2:o3], PR_N - PR_DT)], axis=1).astype(BF16)
    dt_bias128 = _pad_cols(p["dt_bias"].reshape(1, 2 * SSD_HEADS), LANES)
    a_log128 = _pad_cols(p["a_log"].reshape(1, 2 * SSD_HEADS), LANES)
    dskip = jnp.repeat(p["d_skip"], SSD_HEADDIM).reshape(1, SSD_WIDTH)
    norm_w = p["norm_w"].reshape(1, SSD_WIDTH)
    conv_b = p["conv_b"].reshape(1, SSD_CONV_DIM)
    w_out = p["w_out"].astype(BF16)
    cc, sc = _dft_tables(FNET_GW)
    cs = jnp.asarray(np.concatenate([cc, sc], axis=1), BF16)

    def run(x, m, s0):
        L = x.shape[1]
        tm = min(L, 512)
        pr = modmm(x, m[:, 1], m[:, 0], w_pad, tm, 768)
        xbc = conv_silu(pr, p["conv_w"], conv_b)
        y2, s_fin = ssd_scan(xbc, pr, dt_bias128, a_log128, s0)
        cl, sl = _dft_tables(L)
        wpos = jnp.asarray(np.concatenate([cl, -sl], axis=1), BF16)
        xcs = fnet_chan(pr, cs, tm).reshape(bsz, 2 * L, FNET_WIDTH)
        fmix = fnet_pos(wpos, xcs, min(L, 512), 512)
        x_new, h2 = even_out(y2, xbc, pr, fmix, dskip, norm_w, w_out, x, m[:, 2], p["ln_g"], p["ln_b"],
                             m[:, 4], m[:, 3], min(L, 256))
        return x_new, h2, s_fin

    s0 = jnp.zeros((2, bsz, SSD_GROUPS, SSD_STATE, SSD_HPG * SSD_HEADDIM), F32)
    xc_new, h2c, s_ctx = run(x_ctx, mc, s0)
    xl_new, h2l, _ = run(x_lat, ml, s_ctx)
    return xl_new, h2l, xc_new, h2c


def odd_layer(x_lat, x_ctx, ml, mc, p, need_ctx):
    bsz = x_lat.shape[0]
    D = D_MODEL
    mu = p["mu"]
    mu6 = jnp.stack([mu[0], mu[2], mu[3], mu[1], mu[4], mu[5]]).reshape(6, 1, D)
    w_rkv = p["w_rkv"]
    w1cat = _pad_cols(jnp.concatenate([p["w1"][0], p["w1"][1]], axis=1), LORA_PAD)
    a1cat = _pad_cols(jnp.concatenate([p["a1"][0], p["a1"][1]], axis=1), LORA_PAD)
    wcat = jnp.concatenate([w_rkv[0], w_rkv[1], w_rkv[2], w1cat, a1cat, p["g1"]], axis=1).astype(BF16)

    def lora2(w2):
        z = jnp.zeros((2, LORA_PAD, D), F32)
        z = z.at[0, 0:w2.shape[1]].set(w2[0])
        z = z.at[1, w2.shape[1]:2 * w2.shape[1]].set(w2[1])
        return z.astype(BF16)

    w2z = lora2(p["w2"])
    a2z = lora2(p["a2"])
    g2 = p["g2"].astype(BF16)
    w0 = p["w0"].reshape(2, 1, D)
    a0 = p["a0"].reshape(2, 1, D)
    k_k = p["k_k"].reshape(1, D)
    k_a = p["k_a"].reshape(1, D)
    r_k = p["r_k"].reshape(1, D)
    lnx_w = p["lnx_w"].reshape(1, D)
    lnx_b = p["lnx_b"].reshape(1, D)
    w_o = p["w_o"].astype(BF16)

    def run(x, m, s0, grid_w, need_out):
        L = x.shape[1]
        tm = min(L, 512)
        proj = rwkv_proj(x, m[:, 1], m[:, 0], mu6, wcat, tm, 256, grid_w)
        r, kk, v, bonus, g, lw, kd, ka = rwkv_post(proj, w2z, a2z, g2, w0, a0, k_k, k_a, r_k, min(L, 256))
        y2, s_fin = wkv_scan(r, kk, v, lw, kd, ka, s0)
        if not need_out:
            return None, None, s_fin
        x_new, h2 = odd_out(y2, bonus, g, lnx_w, lnx_b, w_o, x, m[:, 2], p["ln_g"], p["ln_b"],
                            m[:, 4], m[:, 3], min(L, 256))
        return x_new, h2, s_fin

    s0 = jnp.zeros((2, bsz, RWKV_PAIRS, LANES, LANES), F32)
    xc_new, h2c, s_ctx = run(x_ctx, mc, s0, None, need_ctx)
    xl_new, h2l, _ = run(x_lat, ml, s_ctx, 64, True)
    return xl_new, h2l, xc_new, h2c


def kernel(x, c, ctx, c_ctx, w_mod, b_mod, ln_g, ln_b, ssd_w_in, ssd_conv_w, ssd_conv_b, ssd_a_log, ssd_dt_bias, ssd_d, ssd_norm_w, even_w_out, rwkv_mu, rwkv_w_rkv, rwkv_w_o, rwkv_w0, rwkv_w1, rwkv_w2, rwkv_a0, rwkv_a1, rwkv_a2, rwkv_g1, rwkv_g2, rwkv_k_k, rwkv_k_a, rwkv_r_k, rwkv_lnx_w, rwkv_lnx_b, router_w, router_bias, moe_w_gate, moe_w_up, moe_w_down):
    bsz, L, D = x.shape
    Lc = ctx.shape[1]
    n_lat = bsz * L
    router_w128 = _pad_cols(router_w, LANES)
    router_b128 = jnp.pad(router_bias.reshape(1, N_EXPERTS), ((0, 0), (0, LANES - N_EXPERTS)),
                          constant_values=-jnp.inf)
    cc = jnp.concatenate([c, c_ctx[None, :], jnp.zeros((7, D), F32)], axis=0)
    x_lat, x_ctx = x, ctx
    for i in range(DEPTH):
        last = i == DEPTH - 1
        j = i // 2
        m_all = dense_silu(cc, w_mod[i], b_mod[i].reshape(1, 6 * D))
        ml = m_all[:bsz].reshape(bsz, 6, 1, D)
        mc = jnp.broadcast_to(m_all[bsz].reshape(1, 6, 1, D), (bsz, 6, 1, D))
        lng0, lnb0 = ln_g[i, 0].reshape(1, D), ln_b[i, 0].reshape(1, D)
        lng1, lnb1 = ln_g[i, 1].reshape(1, D), ln_b[i, 1].reshape(1, D)
        if i % 2 == 0:
            p = dict(w_in=ssd_w_in[j], conv_w=ssd_conv_w[j], conv_b=ssd_conv_b[j], a_log=ssd_a_log[j],
                     dt_bias=ssd_dt_bias[j], d_skip=ssd_d[j], norm_w=ssd_norm_w[j], w_out=even_w_out[j],
                     ln_g=lng0, ln_b=lnb0)
            xl, h2l, xc, h2c = even_layer(x_lat, x_ctx, ml, mc, p)
        else:
            p = dict(mu=rwkv_mu[j], w_rkv=rwkv_w_rkv[j], w_o=rwkv_w_o[j], w0=rwkv_w0[j], w1=rwkv_w1[j],
                     w2=rwkv_w2[j], a0=rwkv_a0[j], a1=rwkv_a1[j], a2=rwkv_a2[j], g1=rwkv_g1[j],
                     g2=rwkv_g2[j], k_k=rwkv_k_k[j], k_a=rwkv_k_a[j], r_k=rwkv_r_k[j],
                     lnx_w=rwkv_lnx_w[j], lnx_b=rwkv_lnx_b[j], ln_g=lng0, ln_b=lnb0)
            xl, h2l, xc, h2c = odd_layer(x_lat, x_ctx, ml, mc, p, not last)
        tm = 256
        if last:
            gates = ml[:, 5]
            out = moe_layer(h2l.reshape(n_lat, D), xl.reshape(n_lat, D), gates,
                            lambda t: t // (L // tm), router_w128, router_b128,
                            moe_w_gate[i], moe_w_up[i], moe_w_down[i], lng1, lnb1, tm)
            x_lat = out.reshape(bsz, L, D)
        else:
            h2 = jnp.concatenate([h2l.reshape(n_lat, D), h2c.reshape(bsz * Lc, D)], axis=0)
            xr = jnp.concatenate([xl.reshape(n_lat, D), xc.reshape(bsz * Lc, D)], axis=0)
            gates = jnp.concatenate([ml[:, 5], mc[:1, 5]], axis=0)
            n_lat_tiles = n_lat // tm
            out = moe_layer(h2, xr, gates,
                            lambda t: jnp.where(t < n_lat_tiles, t // (L // tm), bsz), router_w128, router_b128,
                            moe_w_gate[i], moe_w_up[i], moe_w_down[i], lng1, lnb1, tm)
            x_lat = out[:n_lat].reshape(bsz, L, D)
            x_ctx = out[n_lat:].reshape(bsz, Lc, D)
    return x_lat
```

```python
import functools
import math

import numpy as np
import jax
import jax.numpy as jnp
from jax import lax
from jax.experimental import pallas as pl
from jax.experimental.pallas import tpu as pltpu

F32 = jnp.float32
BF16 = jnp.bfloat16

V7X_VMEM_BYTES = 64 * 1024 * 1024
VMEM_LIMIT = V7X_VMEM_BYTES - 8 * 1024 * 1024
LANES = 128

D_MODEL = 2048
DEPTH = 2
ALPHA = (2 * DEPTH) ** 0.25
LN_EPS = 1e-5
GN_EPS = 64e-5

SSD_WIDTH = 1024
SSD_HEADDIM = 64
SSD_HEADS = 16
SSD_GROUPS = 2
SSD_HPG = 8
SSD_STATE = 128
SSD_CONV_DIM = SSD_WIDTH + 2 * SSD_GROUPS * SSD_STATE
FNET_WIDTH = 1024
FNET_GROUPS = 4
FNET_GW = 256
SSD_Q = 128
PR_Z, PR_XBC, PR_F, PR_DT, PR_N = 0, 1024, 2560, 3584, 3840

RWKV_HEADSIZE = 64
RWKV_HEADS = 32
RWKV_PAIRS = 16
DECAY_LORA = 96
AAA_LORA = 96
GATE_LORA = 256
LORA_PAD = 256
RW_N = 3 * D_MODEL + 3 * LORA_PAD
WKV_T = 64

N_EXPERTS = 32
N_GROUPS = 4
EXPERTS_PER_GROUP = 8
TOP_K = 2
D_EXPERT = 512
MOE_BM = 256


def _cparams(sem):
    return pltpu.CompilerParams(dimension_semantics=sem, vmem_limit_bytes=VMEM_LIMIT)


def _dot(a, b):
    return jnp.dot(a, b, preferred_element_type=F32)


def _dot_nt(a, b):
    return lax.dot_general(a, b, (((1,), (1,)), ((), ())), preferred_element_type=F32)


def _dot_tn(a, b):
    return lax.dot_general(a, b, (((0,), (0,)), ((), ())), preferred_element_type=F32)


def _split2(a):
    hi = a.astype(BF16)
    lo = (a - hi.astype(F32)).astype(BF16)
    return hi, lo


def _split3(a):
    hi = a.astype(BF16)
    r1 = a - hi.astype(F32)
    mid = r1.astype(BF16)
    lo = (r1 - mid.astype(F32)).astype(BF16)
    return hi, mid, lo


def _dot_exact_lhs(a_bf16, b_f32):
    b1, b2, b3 = _split3(b_f32)
    return _dot(a_bf16, b1) + _dot(a_bf16, b2) + _dot(a_bf16, b3)


def _dot_exact_rhs(a_f32, b_bf16):
    a1, a2, a3 = _split3(a_f32)
    return _dot(a1, b_bf16) + _dot(a2, b_bf16) + _dot(a3, b_bf16)


def _dot_x3(a_f32, b_f32):
    a1, a2 = _split2(a_f32)
    b1, b2 = _split2(b_f32)
    return _dot(a1, b1) + _dot(a1, b2) + _dot(a2, b1)


def _silu(x):
    return x * jax.nn.sigmoid(x)


def _softplus(x):
    return jnp.maximum(x, 0.0) + jnp.log(1.0 + jnp.exp(-jnp.abs(x)))


def _layer_norm(v, g, b):
    mu = jnp.mean(v, axis=-1, keepdims=True)
    d = v - mu
    var = jnp.mean(d * d, axis=-1, keepdims=True)
    return d * lax.rsqrt(var + LN_EPS) * g + b


def _dense_silu_kernel(x_ref, w_ref, b_ref, o_ref):
    x = _silu(x_ref[...]).astype(BF16)
    o_ref[...] = _dot(x, w_ref[...].astype(BF16)) + b_ref[...]


def dense_silu(x, w, bias, tn=512):
    m, k = x.shape
    n = w.shape[1]
    return pl.pallas_call(
        _dense_silu_kernel,
        grid=(n // tn,),
        in_specs=[pl.BlockSpec((m, k), lambda j: (0, 0)),
                  pl.BlockSpec((k, tn), lambda j: (0, j)),
                  pl.BlockSpec((1, tn), lambda j: (0, j))],
        out_specs=pl.BlockSpec((m, tn), lambda j: (0, j)),
        out_shape=jax.ShapeDtypeStruct((m, n), F32),
        compiler_params=_cparams(("parallel",)),
        name="dense_silu",
    )(x, w, bias)


def _modmm_kernel(x_ref, sc_ref, sh_ref, w_ref, o_ref, h_ref):
    @pl.when(pl.program_id(2) == 0)
    def _():
        h_ref[...] = (x_ref[...] * (1.0 + sc_ref[...]) + sh_ref[...]).astype(BF16)

    o_ref[...] = _dot(h_ref[...], w_ref[...]).astype(o_ref.dtype)


def modmm(x, sc, sh, w, tm, tn, out_dtype=F32):
    b, L, k = x.shape
    n = w.shape[1]
    return pl.pallas_call(
        _modmm_kernel,
        grid=(b, L // tm, n // tn),
        in_specs=[pl.BlockSpec((None, tm, k), lambda bi, i, j: (bi, i, 0)),
                  pl.BlockSpec((None, 1, k), lambda bi, i, j: (bi, 0, 0)),
                  pl.BlockSpec((None, 1, k), lambda bi, i, j: (bi, 0, 0)),
                  pl.BlockSpec((k, tn), lambda bi, i, j: (0, j))],
        out_specs=pl.BlockSpec((None, tm, tn), lambda bi, i, j: (bi, i, j)),
        out_shape=jax.ShapeDtypeStruct((b, L, n), out_dtype),
        scratch_shapes=[pltpu.VMEM((tm, k), BF16)],
        compiler_params=_cparams(("parallel", "parallel", "arbitrary")),
        name="modmm",
    )(x, sc, sh, w)


def _conv_silu_kernel(x_ref, w_ref, b_ref, o_ref):
    x = x_ref[...]
    L = x.shape[0]
    row = lax.broadcasted_iota(jnp.int32, x.shape, 0)
    prev = jnp.where(row == 0, 0.0, pltpu.roll(x, 1, axis=0))
    nxt = jnp.where(row == L - 1, 0.0, pltpu.roll(x, L - 1, axis=0))
    w = w_ref[...]
    y = prev * w[0:1] + x * w[1:2] + nxt * w[2:3] + b_ref[...]
    o_ref[...] = _silu(y)


def conv_silu(pr, conv_w, conv_b, tc=512):
    b, L, _ = pr.shape
    off = PR_XBC // tc
    return pl.pallas_call(
        _conv_silu_kernel,
        grid=(b, SSD_CONV_DIM // tc),
        in_specs=[pl.BlockSpec((None, L, tc), lambda bi, j: (bi, 0, off + j)),
                  pl.BlockSpec((3, tc), lambda bi, j: (0, j)),
                  pl.BlockSpec((1, tc), lambda bi, j: (0, j))],
        out_specs=pl.BlockSpec((None, L, tc), lambda bi, j: (bi, 0, j)),
        out_shape=jax.ShapeDtypeStruct((b, L, SSD_CONV_DIM), F32),
        compiler_params=_cparams(("parallel", "parallel")),
        name="conv_silu",
    )(pr, conv_w, conv_b)


def _ssd_kernel(x_ref, bc_ref, dtr_ref, dtb_ref, alog_ref, s0_ref, y_ref, sfin_ref, st_ref):
    Q = SSD_Q
    d = pl.program_id(0)
    c = pl.program_id(2)
    nc = pl.num_programs(2)
    fwd = d == 0

    @pl.when(c == 0)
    def _():
        st_ref[...] = s0_ref[...]

    row = lax.broadcasted_iota(jnp.int32, (Q, Q), 0)
    col = lax.broadcasted_iota(jnp.int32, (Q, Q), 1)
    sgn = 1 - 2 * d
    incl = (row - col) * sgn >= 0
    tri = jnp.where(incl, 1.0, 0.0).astype(BF16)

    dt_all = _softplus(dtr_ref[:, 0:LANES] + dtb_ref[...])
    dA_all = dt_all * (-jnp.exp(alog_ref[...]))
    acum_all = _dot_exact_lhs(tri, dA_all)
    acum_t = acum_all.T

    lane_h = lax.broadcasted_iota(jnp.int32, (LANES, SSD_HPG * SSD_HEADDIM), 0)
    col_h = lax.broadcasted_iota(jnp.int32, (LANES, SSD_HPG * SSD_HEADDIM), 1) // SSD_HEADDIM
    lane128 = lax.broadcasted_iota(jnp.int32, (Q, LANES), 1)
    m_lo = lane128 < SSD_HEADDIM

    for g in range(SSD_GROUPS):
        expand = jnp.where(lane_h == d * SSD_HEADS + g * SSD_HPG + col_h, 1.0, 0.0).astype(BF16)
        dt_e = _dot_exact_rhs(dt_all, expand)
        a_e = _dot_exact_rhs(acum_all, expand)
        a_tot = jnp.where(fwd, a_e[Q - 1:Q], a_e[0:1])
        xg = x_ref[:, g * 512:(g + 1) * 512]
        xdt = xg * dt_e
        bg = bc_ref[:, g * SSD_STATE:(g + 1) * SSD_STATE]
        cg = bc_ref[:, SSD_GROUPS * SSD_STATE + g * SSD_STATE:SSD_GROUPS * SSD_STATE + (g + 1) * SSD_STATE]
        bgb = bg.astype(BF16)
        cgb = cg.astype(BF16)
        scores = _dot_nt(cgb, bgb)
        st = st_ref[g]
        y_off = _dot(cgb, st.astype(BF16)) * jnp.exp(a_e)
        xdt_b = xdt.astype(BF16)
        for pr_i in range(SSD_HPG // 2):
            xp = xdt_b[:, pr_i * LANES:(pr_i + 1) * LANES]
            acc = y_off[:, pr_i * LANES:(pr_i + 1) * LANES]
            for half in range(2):
                r = 2 * pr_i + half
                hrow = g * SSD_HPG + r
                a_col = jnp.where(fwd, acum_all[:, hrow:hrow + 1],
                                  acum_all[:, SSD_HEADS + hrow:SSD_HEADS + hrow + 1])
                a_row = jnp.where(fwd, acum_t[hrow:hrow + 1, :],
                                  acum_t[SSD_HEADS + hrow:SSD_HEADS + hrow + 1, :])
                seg = jnp.minimum(a_col - a_row, 0.0)
                m = jnp.where(incl, scores * jnp.exp(seg), 0.0).astype(BF16)
                xh = jnp.where(m_lo if half == 0 else jnp.logical_not(m_lo), xp, jnp.zeros_like(xp))
                acc = acc + _dot(m, xh)
            y_ref[:, g * 512 + pr_i * LANES:g * 512 + (pr_i + 1) * LANES] = acc
        xde = (xdt * jnp.exp(a_tot - a_e)).astype(BF16)
        st_ref[g] = st * jnp.exp(a_tot) + _dot_tn(bgb, xde)

    @pl.when(c == nc - 1)
    def _():
        sfin_ref[...] = st_ref[...]


def ssd_scan(xbc, pr, dt_bias128, a_log128, s0):
    b, L, _ = xbc.shape
    Q = SSD_Q
    nc = L // Q

    def cidx(d, c):
        return jnp.where(d == 0, c, nc - 1 - c)

    return pl.pallas_call(
        _ssd_kernel,
        grid=(2, b, nc),
        in_specs=[pl.BlockSpec((None, Q, SSD_WIDTH), lambda d, bi, c: (bi, cidx(d, c), 0)),
                  pl.BlockSpec((None, Q, 512), lambda d, bi, c: (bi, cidx(d, c), SSD_WIDTH // 512)),
                  pl.BlockSpec((None, Q, 256), lambda d, bi, c: (bi, cidx(d, c), PR_DT // 256)),
                  pl.BlockSpec((1, LANES), lambda d, bi, c: (0, 0)),
                  pl.BlockSpec((1, LANES), lambda d, bi, c: (0, 0)),
                  pl.BlockSpec((None, None, SSD_GROUPS, SSD_STATE, 512), lambda d, bi, c: (d, bi, 0, 0, 0))],
        out_specs=[pl.BlockSpec((None, None, Q, SSD_WIDTH), lambda d, bi, c: (d, bi, cidx(d, c), 0)),
                   pl.BlockSpec((None, None, SSD_GROUPS, SSD_STATE, 512), lambda d, bi, c: (d, bi, 0, 0, 0))],
        out_shape=[jax.ShapeDtypeStruct((2, b, L, SSD_WIDTH), F32),
                   jax.ShapeDtypeStruct((2, b, SSD_GROUPS, SSD_STATE, 512), F32)],
        scratch_shapes=[pltpu.VMEM((SSD_GROUPS, SSD_STATE, 512), F32)],
        compiler_params=_cparams(("parallel", "parallel", "arbitrary")),
        name="ssd_scan",
    )(xbc, xbc, pr, dt_bias128, a_log128, s0)


def _fnet_chan_kernel(f_ref, cs_ref, o_ref):
    res = _dot(f_ref[...].astype(BF16), cs_ref[...])
    o_ref[0] = res[:, :FNET_GW].astype(o_ref.dtype)
    o_ref[1] = res[:, FNET_GW:].astype(o_ref.dtype)


def fnet_chan(pr, cs, tm):
    b, L, _ = pr.shape
    off = PR_F // FNET_GW
    return pl.pallas_call(
        _fnet_chan_kernel,
        grid=(b, L // tm, FNET_GROUPS),
        in_specs=[pl.BlockSpec((None, tm, FNET_GW), lambda bi, i, g: (bi, i, off + g)),
                  pl.BlockSpec((FNET_GW, 2 * FNET_GW), lambda bi, i, g: (0, 0))],
        out_specs=pl.BlockSpec((None, 2, tm, FNET_GW), lambda bi, i, g: (bi, 0, i, g)),
        out_shape=jax.ShapeDtypeStruct((b, 2, L, FNET_WIDTH), BF16),
        compiler_params=_cparams(("parallel", "parallel", "parallel")),
        name="fnet_chan",
    )(pr, cs)


def _fnet_pos_kernel(w_ref, x_ref, o_ref):
    o_ref[...] = _dot(w_ref[...], x_ref[...])


def fnet_pos(wpos, xcs, tm, tn):
    b, k, n = xcs.shape
    L = wpos.shape[0]
    return pl.pallas_call(
        _fnet_pos_kernel,
        grid=(b, n // tn, L // tm),
        in_specs=[pl.BlockSpec((tm, k), lambda bi, j, i: (i, 0)),
                  pl.BlockSpec((None, k, tn), lambda bi, j, i: (bi, 0, j))],
        out_specs=pl.BlockSpec((None, tm, tn), lambda bi, j, i: (bi, i, j)),
        out_shape=jax.ShapeDtypeStruct((b, L, n), F32),
        compiler_params=_cparams(("parallel", "parallel", "parallel")),
        name="fnet_pos",
    )(wpos, xcs)


def _dft_tables(n):
    k = np.arange(n, dtype=np.int64)
    ang = 2.0 * np.pi * ((k[:, None] * k[None, :]) % n).astype(np.float64) / n
    s = 1.0 / math.sqrt(n)
    return np.cos(ang) * s, np.sin(ang) * s


def _even_out_kernel(y0_ref, y1_ref, xs_ref, z_ref, f_ref, dsk_ref, nw_ref, w_ref, xres_ref, gate_ref,
                     lng_ref, lnb_ref, sc_ref, sh_ref, xo_ref, h2_ref, lhs_ref):
    y = y0_ref[...] + y1_ref[...] + dsk_ref[...] * xs_ref[...]
    u = y * _silu(z_ref[...])
    gw = SSD_WIDTH // SSD_GROUPS
    for g in range(SSD_GROUPS):
        ug = u[:, g * gw:(g + 1) * gw]
        ms = jnp.mean(ug * ug, axis=-1, keepdims=True)
        lhs_ref[:, g * gw:(g + 1) * gw] = (ug * lax.rsqrt(ms + LN_EPS) * nw_ref[:, g * gw:(g + 1) * gw]).astype(BF16)
    lhs_ref[:, SSD_WIDTH:] = f_ref[...].astype(BF16)
    ymix = _dot(lhs_ref[...], w_ref[...])
    xn = _layer_norm(ALPHA * xres_ref[...] + gate_ref[...] * ymix, lng_ref[...], lnb_ref[...])
    xo_ref[...] = xn
    h2_ref[...] = xn * (1.0 + sc_ref[...]) + sh_ref[...]


def even_out(y2, xbc, pr, fmix, dskip, norm_w, w_out, x_res, gate, ln_g, ln_b, sc2, sh2, tm):
    b, L, D = x_res.shape
    vec = lambda: pl.BlockSpec((1, D), lambda bi, i: (0, 0))
    bvec = lambda: pl.BlockSpec((None, 1, D), lambda bi, i: (bi, 0, 0))
    return pl.pallas_call(
        _even_out_kernel,
        grid=(b, L // tm),
        in_specs=[pl.BlockSpec((None, None, tm, SSD_WIDTH), lambda bi, i: (0, bi, i, 0)),
                  pl.BlockSpec((None, None, tm, SSD_WIDTH), lambda bi, i: (1, bi, i, 0)),
                  pl.BlockSpec((None, tm, SSD_WIDTH), lambda bi, i: (bi, i, 0)),
                  pl.BlockSpec((None, tm, SSD_WIDTH), lambda bi, i: (bi, i, PR_Z // SSD_WIDTH)),
                  pl.BlockSpec((None, tm, FNET_WIDTH), lambda bi, i: (bi, i, 0)),
                  pl.BlockSpec((1, SSD_WIDTH), lambda bi, i: (0, 0)),
                  pl.BlockSpec((1, SSD_WIDTH), lambda bi, i: (0, 0)),
                  pl.BlockSpec((D, D), lambda bi, i: (0, 0)),
                  pl.BlockSpec((None, tm, D), lambda bi, i: (bi, i, 0)),
                  bvec(), vec(), vec(), bvec(), bvec()],
        out_specs=[pl.BlockSpec((None, tm, D), lambda bi, i: (bi, i, 0)),
                   pl.BlockSpec((None, tm, D), lambda bi, i: (bi, i, 0))],
        out_shape=[jax.ShapeDtypeStruct((b, L, D), F32), jax.ShapeDtypeStruct((b, L, D), F32)],
        scratch_shapes=[pltpu.VMEM((tm, D), BF16)],
        compiler_params=_cparams(("parallel", "parallel")),
        name="even_out",
    )(y2, y2, xbc, pr, fmix, dskip, norm_w, w_out, x_res, gate, ln_g, ln_b, sc2, sh2)


def _router_kernel(h_ref, w_ref, bias_ref, o_ref):
    logits = _dot_x3(h_ref[...], w_ref[...])
    s = jax.nn.sigmoid(logits)
    lane = lax.broadcasted_iota(jnp.int32, s.shape, 1)
    neg = jnp.float32(-jnp.inf)
    ssel = s + bias_ref[...]
    big = jnp.int32(4 * LANES)

    def top2(mask):
        v = jnp.where(mask, ssel, neg)
        m1 = jnp.max(v, axis=-1, keepdims=True)
        i1 = jnp.min(jnp.where(v == m1, lane, big), axis=-1, keepdims=True)
        v2 = jnp.where(lane == i1, neg, v)
        m2 = jnp.max(v2, axis=-1, keepdims=True)
        i2 = jnp.min(jnp.where(v2 == m2, lane, big), axis=-1, keepdims=True)
        return m1 + m2, i1, i2

    best, bi1, bi2 = top2(lane < EXPERTS_PER_GROUP)
    for g in range(1, N_GROUPS):
        sc, i1, i2 = top2((lane >= g * EXPERTS_PER_GROUP) & (lane < (g + 1) * EXPERTS_PER_GROUP))
        better = sc > best
        best = jnp.where(better, sc, best)
        bi1 = jnp.where(better, i1, bi1)
        bi2 = jnp.where(better, i2, bi2)
    s1 = jnp.sum(jnp.where(lane == bi1, s, 0.0), axis=-1, keepdims=True)
    s2 = jnp.sum(jnp.where(lane == bi2, s, 0.0), axis=-1, keepdims=True)
    tot = s1 + s2
    out = jnp.where(lane == 0, bi1.astype(F32),
                    jnp.where(lane == 1, bi2.astype(F32),
                              jnp.where(lane == 2, s1 / tot, jnp.where(lane == 3, s2 / tot, 0.0))))
    o_ref[...] = out


def router(h, w128, bias128, tm):
    T, D = h.shape
    return pl.pallas_call(
        _router_kernel,
        grid=(T // tm,),
        in_specs=[pl.BlockSpec((tm, D), lambda i: (i, 0)),
                  pl.BlockSpec((D, LANES), lambda i: (0, 0)),
                  pl.BlockSpec((1, LANES), lambda i: (0, 0))],
        out_specs=pl.BlockSpec((tm, LANES), lambda i: (i, 0)),
        out_shape=jax.ShapeDtypeStruct((T, LANES), F32),
        compiler_params=_cparams(("parallel",)),
        name="router",
    )(h, w128, bias128)


def _moe_kernel(be_ref, nused_ref, src_ref, nxt_ref, h_hbm, wg_ref, wu_ref, wd_ref, o_ref,
                xb0, xb1, wgb, wub, wdb, sem):
    i = pl.program_id(0)
    nu = nused_ref[0]
    bm = xb0.shape[0]

    def issue(idx_ref, dst, s):
        for r in range(bm):
            pltpu.make_async_copy(h_hbm.at[pl.ds(idx_ref[0, 0, r], 1)], dst.at[pl.ds(r, 1)], sem.at[s]).start()

    def wait(dst, s):
        pltpu.make_async_copy(h_hbm.at[pl.ds(0, bm)], dst, sem.at[s]).wait()

    @pl.when(i == 0)
    def _():
        issue(src_ref, xb0, 0)

    @pl.when(i < nu)
    def _():
        prev = be_ref[jnp.maximum(i - 1, 0)]

        @pl.when((i == 0) | (be_ref[i] != prev))
        def _():
            wgb[...] = wg_ref[...].astype(BF16)
            wub[...] = wu_ref[...].astype(BF16)
            wdb[...] = wd_ref[...].astype(BF16)

        for par in range(2):
            cur, nxt = (xb0, xb1) if par == 0 else (xb1, xb0)

            @pl.when(i % 2 == par)
            def _(cur=cur, nxt=nxt, par=par):
                wait(cur, par)
                issue(nxt_ref, nxt, 1 - par)
                x = cur[...].astype(BF16)
                hid = _silu(_dot(x, wgb[...])) * _dot(x, wub[...])
                o_ref[...] = _dot(hid.astype(BF16), wdb[...])

                @pl.when(i == nu - 1)
                def _():
                    wait(nxt, 1 - par)

    @pl.when(i >= nu)
    def _():
        o_ref[...] = jnp.zeros_like(o_ref)


def moe_experts(h, block_e, nused, src_tok, w_gate, w_up, w_down):
    T, D = h.shape
    nblk = block_e.shape[0]
    bm = MOE_BM
    last = lambda i, nu: jnp.minimum(i, nu[0] - 1)
    gs = pltpu.PrefetchScalarGridSpec(
        num_scalar_prefetch=2,
        grid=(nblk,),
        in_specs=[pl.BlockSpec((1, 1, bm), lambda i, be, nu: (last(i, nu), 0, 0), memory_space=pltpu.SMEM),
                  pl.BlockSpec((1, 1, bm), lambda i, be, nu: (last(i + 1, nu), 0, 0), memory_space=pltpu.SMEM),
                  pl.BlockSpec(memory_space=pl.ANY),
                  pl.BlockSpec((None, D, D_EXPERT), lambda i, be, nu: (be[last(i, nu)], 0, 0)),
                  pl.BlockSpec((None, D, D_EXPERT), lambda i, be, nu: (be[last(i, nu)], 0, 0)),
                  pl.BlockSpec((None, D_EXPERT, D), lambda i, be, nu: (be[last(i, nu)], 0, 0))],
        out_specs=pl.BlockSpec((bm, D), lambda i, be, nu: (i, 0)),
        scratch_shapes=[pltpu.VMEM((bm, D), F32),
                        pltpu.VMEM((bm, D), F32),
                        pltpu.VMEM((D, D_EXPERT), BF16),
                        pltpu.VMEM((D, D_EXPERT), BF16),
                        pltpu.VMEM((D_EXPERT, D), BF16),
                        pltpu.SemaphoreType.DMA((2,))],
    )
    src3 = src_tok.reshape(nblk, 1, bm)
    return pl.pallas_call(
        _moe_kernel,
        grid_spec=gs,
        out_shape=jax.ShapeDtypeStruct((nblk * bm, D), F32),
        compiler_params=_cparams(("arbitrary",)),
        name="moe_experts",
    )(block_e, nused, src3, src3, h, w_gate, w_up, w_down)


def _moe_combine_kernel(dest_ref, nxt_ref, yb_hbm, rw_ref, x_ref, gate_ref, lng_ref, lnb_ref, o_ref, yb0, yb1, sem):
    i = pl.program_id(0)
    n = pl.num_programs(0)
    tm = x_ref.shape[0]

    def issue(idx_ref, dst, s):
        for r in range(tm):
            pltpu.make_async_copy(yb_hbm.at[pl.ds(idx_ref[0, 0, r], 1)], dst.at[0, pl.ds(r, 1)],
                                  sem.at[s]).start(priority=0)
            pltpu.make_async_copy(yb_hbm.at[pl.ds(idx_ref[0, 0, tm + r], 1)], dst.at[1, pl.ds(r, 1)],
                                  sem.at[s]).start(priority=1)

    def wait(dst, s):
        pltpu.make_async_copy(yb_hbm.at[pl.ds(0, tm)], dst.at[0], sem.at[s]).wait()
        pltpu.make_async_copy(yb_hbm.at[pl.ds(0, tm)], dst.at[1], sem.at[s]).wait()

    @pl.when(i == 0)
    def _():
        issue(dest_ref, yb0, 0)

    for par in range(2):
        cur, nxt = (yb0, yb1) if par == 0 else (yb1, yb0)

        @pl.when(i % 2 == par)
        def _(cur=cur, nxt=nxt, par=par):
            wait(cur, par)
            issue(nxt_ref, nxt, 1 - par)
            rw = rw_ref[...]
            y2 = rw[:, 2:3] * cur[0] + rw[:, 3:4] * cur[1]
            o_ref[...] = _layer_norm(ALPHA * x_ref[...] + gate_ref[...] * y2, lng_ref[...], lnb_ref[...])

            @pl.when(i == n - 1)
            def _():
                wait(nxt, 1 - par)


def moe_combine(dest_tiles, ybuf, rw, x, gates, gate_tile_map, ln_g, ln_b, tm):
    T, D = x.shape
    nt = T // tm
    return pl.pallas_call(
        _moe_combine_kernel,
        grid=(nt,),
        in_specs=[pl.BlockSpec((1, 1, 2 * tm), lambda i: (i, 0, 0), memory_space=pltpu.SMEM),
                  pl.BlockSpec((1, 1, 2 * tm), lambda i: (jnp.minimum(i + 1, nt - 1), 0, 0), memory_space=pltpu.SMEM),
                  pl.BlockSpec(memory_space=pl.ANY),
                  pl.BlockSpec((tm, LANES), lambda i: (i, 0)),
                  pl.BlockSpec((tm, D), lambda i: (i, 0)),
                  pl.BlockSpec((None, 1, D), lambda i: (gate_tile_map(i), 0, 0)),
                  pl.BlockSpec((1, D), lambda i: (0, 0)),
                  pl.BlockSpec((1, D), lambda i: (0, 0))],
        out_specs=pl.BlockSpec((tm, D), lambda i: (i, 0)),
        out_shape=jax.ShapeDtypeStruct((T, D), F32),
        scratch_shapes=[pltpu.VMEM((2, tm, D), F32), pltpu.VMEM((2, tm, D), F32), pltpu.SemaphoreType.DMA((2,))],
        compiler_params=_cparams(("arbitrary",)),
        name="moe_combine",
    )(dest_tiles, dest_tiles, ybuf, rw, x, gates, ln_g, ln_b)


def moe_layer(h2, x_res, gates, gate_tile_map, router_w128, router_b128, w_gate, w_up, w_down, ln_g, ln_b, tm=256):
    T, D = h2.shape
    bm = MOE_BM
    rw = router(h2, router_w128, router_b128, tm)
    idx = rw[:, :TOP_K].astype(jnp.int32)
    e_flat = idx.reshape(-1)
    A = e_flat.shape[0]
    onehot = (e_flat[:, None] == jnp.arange(N_EXPERTS, dtype=jnp.int32)[None, :]).astype(jnp.int32)
    csum = jnp.cumsum(onehot, axis=0)
    counts = csum[-1]
    rank = jnp.sum(onehot * csum, axis=1) - 1
    padded = (counts + bm - 1) // bm * bm
    ends = jnp.cumsum(padded)
    pstart = ends - padded
    dest = pstart[e_flat] + rank
    nblk = (A + N_EXPERTS * (bm - 1) + bm - 1) // bm
    P = nblk * bm
    tok = jnp.arange(A, dtype=jnp.int32) // TOP_K
    src_tok = jnp.zeros((P,), jnp.int32).at[dest].set(tok, unique_indices=True, mode="promise_in_bounds")
    blk_start = jnp.arange(nblk, dtype=jnp.int32) * bm
    block_e = jnp.minimum(jnp.sum((ends[None, :] <= blk_start[:, None]).astype(jnp.int32), axis=1),
                          N_EXPERTS - 1).astype(jnp.int32)
    nused = (ends[-1] // bm).astype(jnp.int32).reshape(1)
    ybuf = moe_experts(h2, block_e, nused, src_tok, w_gate, w_up, w_down)
    dest2 = dest.reshape(T // tm, tm, TOP_K)
    dest_tiles = jnp.concatenate([dest2[:, :, 0], dest2[:, :, 1]], axis=1).reshape(T // tm, 1, 2 * tm)
    return moe_combine(dest_tiles.astype(jnp.int32), ybuf, rw, x_res, gates, gate_tile_map, ln_g, ln_b, tm)


def _rwkv_proj_kernel(x_ref, xp_ref, xn_ref, sc_ref, sh_ref, mu_ref, w_ref, o_ref, h_ref, xx_ref, mix_ref,
                      *, grid_w, n_main):
    i = pl.program_id(1)
    j = pl.program_id(2)
    nt = pl.num_programs(1)
    tm, D = h_ref.shape
    q = D // 4

    @pl.when(j == 0)
    def _():
        sc = 1.0 + sc_ref[...]
        sh = sh_ref[...]
        h = x_ref[...] * sc + sh
        h_ref[...] = h
        row = lax.broadcasted_iota(jnp.int32, (tm, q), 0)
        if grid_w is None:
            prev = lambda a: jnp.where(row == 0, 0.0, pltpu.roll(a, 1, axis=0))
            nxt = lambda a: jnp.where(row == tm - 1, 0.0, pltpu.roll(a, tm - 1, axis=0))
            parts = [prev(h[:, 0:q]), nxt(h[:, q:2 * q]), prev(h[:, 2 * q:3 * q]), nxt(h[:, 3 * q:])]
        else:
            wpos = row % grid_w
            left = jnp.where(wpos == 0, 0.0, pltpu.roll(h[:, 0:q], 1, axis=0))
            right = jnp.where(wpos == grid_w - 1, 0.0, pltpu.roll(h[:, q:2 * q], tm - 1, axis=0))
            hp = xp_ref[:, 2 * q:3 * q] * sc[:, 2 * q:3 * q] + sh[:, 2 * q:3 * q]
            hn = xn_ref[:, 3 * q:] * sc[:, 3 * q:] + sh[:, 3 * q:]
            hp = jnp.where(i == 0, 0.0, hp)
            hn = jnp.where(i == nt - 1, 0.0, hn)
            up = jnp.concatenate([hp, h[:tm - grid_w, 2 * q:3 * q]], axis=0)
            down = jnp.concatenate([h[grid_w:, 3 * q:], hn], axis=0)
            parts = [left, right, up, down]
        for k in range(4):
            xx_ref[:, k * q:(k + 1) * q] = parts[k] - h[:, k * q:(k + 1) * q]

    is_main = j < 3 * n_main
    new_stream = jnp.where(is_main, j % n_main == 0, True)

    @pl.when(new_stream)
    def _():
        mix_ref[...] = (h_ref[...] + xx_ref[...] * mu_ref[...]).astype(BF16)

    o_ref[...] = _dot(mix_ref[...], w_ref[...]).astype(o_ref.dtype)


def rwkv_proj(x, sc, sh, mu6, wcat, tm, tn, grid_w):
    b, L, D = x.shape
    n = wcat.shape[1]
    n_main = D // tn
    nj = n // tn
    gw = 64 if grid_w is None else grid_w
    nh = L // gw

    def stream(j):
        return jnp.where(j < 3 * n_main, j // n_main, 3 + (j - 3 * n_main) // (LORA_PAD // tn))

    kern = functools.partial(_rwkv_proj_kernel, grid_w=grid_w, n_main=n_main)
    return pl.pallas_call(
        kern,
        grid=(b, L // tm, nj),
        in_specs=[pl.BlockSpec((None, tm, D), lambda bi, i, j: (bi, i, 0)),
                  pl.BlockSpec((None, gw, D), lambda bi, i, j: (bi, jnp.maximum(i * (tm // gw) - 1, 0), 0)),
                  pl.BlockSpec((None, gw, D), lambda bi, i, j: (bi, jnp.minimum((i + 1) * (tm // gw), nh - 1), 0)),
                  pl.BlockSpec((None, 1, D), lambda bi, i, j: (bi, 0, 0)),
                  pl.BlockSpec((None, 1, D), lambda bi, i, j: (bi, 0, 0)),
                  pl.BlockSpec((None, 1, D), lambda bi, i, j: (stream(j), 0, 0)),
                  pl.BlockSpec((D, tn), lambda bi, i, j: (0, j))],
        out_specs=pl.BlockSpec((None, tm, tn), lambda bi, i, j: (bi, i, j)),
        out_shape=jax.ShapeDtypeStruct((b, L, n), BF16),
        scratch_shapes=[pltpu.VMEM((tm, D), F32), pltpu.VMEM((tm, D), F32), pltpu.VMEM((tm, D), BF16)],
        compiler_params=_cparams(("parallel", "parallel", "arbitrary")),
        name="rwkv_proj",
    )(x, x, x, sc, sh, mu6, wcat)


def _seg_sum64(x, ones_blk):
    hi, lo = _split2(x)
    return _dot(hi, ones_blk) + _dot(lo, ones_blk)


def _rwkv_post_kernel(r_ref, k_ref, v_ref, hw_ref, ha_ref, hg_ref, w2_ref, a2_ref, g2_ref,
                      w0_ref, a0_ref, kk_ref, ka_ref, rk_ref,
                      ro_ref, kko_ref, vo_ref, bo_ref, go_ref, lw_ref, kd_ref, kao_ref):
    r = r_ref[...].astype(F32)
    k = k_ref[...].astype(F32)
    v = v_ref[...].astype(F32)
    tm, D = r.shape
    li = lax.broadcasted_iota(jnp.int32, (LANES, LANES), 0) // RWKV_HEADSIZE
    lj = lax.broadcasted_iota(jnp.int32, (LANES, LANES), 1) // RWKV_HEADSIZE
    ones_blk = jnp.where(li == lj, 1.0, 0.0).astype(BF16)
    g = _dot(jax.nn.sigmoid(hg_ref[...].astype(F32)).astype(BF16), g2_ref[...])
    thw = jnp.tanh(hw_ref[...].astype(F32)).astype(BF16)
    ha = ha_ref[...]
    kkr = k * kk_ref[...]
    rkr = r * k * rk_ref[...]
    aas = []
    for e in range(2):
        dw = _dot(thw, w2_ref[e])
        wl = -_softplus(-(w0_ref[e] + dw)) - 0.5
        lw = -jnp.exp(wl)
        a = jax.nn.sigmoid(a0_ref[e] + _dot(ha, a2_ref[e]))
        aas.append(a)
        kd = k * (1.0 + (a - 1.0) * ka_ref[...])
        for p in range(RWKV_PAIRS):
            sl = slice(p * LANES, (p + 1) * LANES)
            lw_ref[e, p] = lw[:, sl]
            kd_ref[e, p] = kd[:, sl].astype(kd_ref.dtype)
    for p in range(RWKV_PAIRS):
        sl = slice(p * LANES, (p + 1) * LANES)
        kp = kkr[:, sl]
        nrm = jnp.maximum(jnp.sqrt(_seg_sum64(kp * kp, ones_blk)), 1e-12)
        kkp = kp / nrm
        kko_ref[p] = kkp.astype(kko_ref.dtype)
        for e in range(2):
            kao_ref[e, p] = (kkp * aas[e][:, sl]).astype(kao_ref.dtype)
        bo_ref[p] = (_seg_sum64(rkr[:, sl], ones_blk) * v[:, sl]).astype(bo_ref.dtype)
        ro_ref[p] = r[:, sl].astype(ro_ref.dtype)
        vo_ref[p] = v[:, sl].astype(vo_ref.dtype)
        go_ref[p] = g[:, sl].astype(go_ref.dtype)


def rwkv_post(proj, w2z, a2z, g2, w0, a0, k_k, k_a, r_k, tm):
    b, L, _ = proj.shape
    D = D_MODEL
    NP = RWKV_PAIRS
    col = lambda off, w: pl.BlockSpec((None, tm, w), lambda bi, i: (bi, i, off // w))
    full = lambda shape: pl.BlockSpec(shape, lambda bi, i: (0,) * len(shape))
    pair = lambda: pl.BlockSpec((None, NP, tm, LANES), lambda bi, i: (bi, 0, i, 0))
    pair2 = lambda: pl.BlockSpec((2, None, NP, tm, LANES), lambda bi, i: (0, bi, 0, i, 0))
    sh1 = lambda dt: jax.ShapeDtypeStruct((b, NP, L, LANES), dt)
    sh2 = lambda dt: jax.ShapeDtypeStruct((2, b, NP, L, LANES), dt)
    return pl.pallas_call(
        _rwkv_post_kernel,
        grid=(b, L // tm),
        in_specs=[col(0, D), col(D, D), col(2 * D, D),
                  col(3 * D, LORA_PAD), col(3 * D + LORA_PAD, LORA_PAD), col(3 * D + 2 * LORA_PAD, LORA_PAD),
                  full((2, LORA_PAD, D)), full((2, LORA_PAD, D)), full((LORA_PAD, D)),
                  full((2, 1, D)), full((2, 1, D)), full((1, D)), full((1, D)), full((1, D))],
        out_specs=[pair(), pair(), pair(), pair(), pair(), pair2(), pair2(), pair2()],
        out_shape=[sh1(BF16), sh1(BF16), sh1(BF16), sh1(BF16), sh1(BF16), sh2(F32), sh2(BF16), sh2(BF16)],
        compiler_params=_cparams(("parallel", "parallel")),
        name="rwkv_post",
    )(proj, proj, proj, proj, proj, proj, w2z, a2z, g2, w0, a0, k_k, k_a, r_k)


def _wkv_kernel(r_ref, kk_ref, v_ref, lw_ref, kd_ref, ka_ref, s0_ref, y_ref, sfin_ref, st_ref):
    T = WKV_T
    d = pl.program_id(0)
    c = pl.program_id(2)
    nc = pl.num_programs(2)
    fwd = d == 0

    @pl.when(c == 0)
    def _():
        st_ref[...] = s0_ref[...]

    row = lax.broadcasted_iota(jnp.int32, (T, T), 0)
    col = lax.broadcasted_iota(jnp.int32, (T, T), 1)
    sgn = 1 - 2 * d
    tri = jnp.where((row - col) * sgn >= 0, 1.0, 0.0).astype(BF16)
    row2 = lax.broadcasted_iota(jnp.int32, (T, 2 * T), 0)
    col2 = lax.broadcasted_iota(jnp.int32, (T, 2 * T), 1) % T
    incl2 = (row2 - col2) * sgn >= 0
    strict2 = (row2 - col2) * sgn > 0
    lane = lax.broadcasted_iota(jnp.int32, (T, LANES), 1)
    m_a = lane < RWKV_HEADSIZE
    bi = lax.broadcasted_iota(jnp.int32, (LANES, LANES), 0) // RWKV_HEADSIZE
    bj = lax.broadcasted_iota(jnp.int32, (LANES, LANES), 1) // RWKV_HEADSIZE
    blockdiag = bi == bj
    eye2 = jnp.where(row2 == col2, 1.0, 0.0)
    blk = []
    size = 8
    while size <= T:
        blk.append(row2 // size == col2 // size)
        size *= 2

    def stack2(z):
        zero = jnp.zeros_like(z)
        return jnp.concatenate([jnp.where(m_a, z, zero), jnp.where(m_a, zero, z)], axis=0)

    def cast(z):
        return z.astype(BF16)

    def pairmul(xb, yb):
        return _dot(xb, stack2(yb))

    def each(f, *ls):
        return [f(*a) for a in zip(*ls)]

    ps = list(range(RWKV_PAIRS))
    lw = [lw_ref[p] for p in ps]
    hi = each(cast, lw)
    mid = each(lambda a, h_: cast(a - h_.astype(F32)), lw, hi)
    cum = each(lambda h_, m_: _dot(tri, h_) + _dot(tri, m_), hi, mid)
    e_neg = each(lambda cm: jnp.exp(-cm), cum)
    at = each(lambda p, cm, l_: -kk_ref[p].astype(F32) * jnp.exp(cm - l_), ps, cum, lw)
    rt = each(lambda p, cm: r_ref[p].astype(F32) * jnp.exp(cm), ps, cum)
    bt = each(lambda p, en: cast(ka_ref[p].astype(F32) * en), ps, e_neg)
    kt = each(lambda p, en: cast(kd_ref[p].astype(F32) * en), ps, e_neg)
    x1 = each(lambda a, r_: cast(jnp.concatenate([a, r_], axis=0)), at, rt)
    x2s = each(lambda b_, k_: jnp.concatenate([stack2(b_), stack2(k_)], axis=0), bt, kt)
    gall = each(_dot_nt, x1, x2s)
    lc = each(lambda g_: jnp.where(strict2, g_[:T, :2 * T], 0.0), gall)
    n1 = each(lambda l_: cast(jnp.where(blk[0], l_, 0.0)), lc)
    n2 = each(lambda a: cast(pairmul(a, a)), n1)
    n4 = each(lambda a: cast(pairmul(a, a)), n2)
    inv = each(lambda a: eye2 + a.astype(F32), n1)
    inv = each(lambda iv, a: iv + pairmul(cast(iv), a), inv, n2)
    inv = each(lambda iv, a: iv + pairmul(cast(iv), a), inv, n4)
    for lvl in range(1, len(blk)):
        offm = blk[lvl] & jnp.logical_not(blk[lvl - 1])
        ivb = each(cast, inv)
        t1 = each(lambda l_, ib: cast(pairmul(cast(jnp.where(offm, l_, 0.0)), ib)), lc, ivb)
        inv = each(lambda iv, ib, t_: iv + pairmul(ib, t_), inv, ivb, t1)
    invb = each(cast, inv)
    lak = each(lambda g_: cast(jnp.where(strict2, g_[:T, 2 * T:], 0.0)), gall)
    rbk = each(lambda g_: cast(jnp.concatenate([jnp.where(incl2, g_[T:, :2 * T], 0.0),
                                                jnp.where(incl2, g_[T:, 2 * T:], 0.0)], axis=1)), gall)
    v = [v_ref[p] for p in ps]
    v2 = each(stack2, v)
    s = [st_ref[p] for p in ps]
    h = each(lambda x_, s_: _dot_nt(x_, cast(s_)), x1, s)
    wv = each(lambda h_, l_, v_: h_[:T] + _dot(l_, v_), h, lak, v2)
    ub = each(lambda ib, w_: cast(pairmul(ib, cast(w_))), invb, wv)
    y = each(lambda h_, rb_, u_, v_: h_[T:] + _dot(rb_, jnp.concatenate([stack2(u_), v_], axis=0)), h, rbk, ub, v2)
    ds = each(lambda u_, v_, b_, k_: _dot_tn(jnp.concatenate([u_, v_], axis=0), jnp.concatenate([b_, k_], axis=0)),
              ub, v, bt, kt)
    for p in ps:
        y_ref[p] = y[p].astype(y_ref.dtype)
        c_end = jnp.where(fwd, cum[p][T - 1:T], cum[p][0:1])
        st_ref[p] = jnp.where(blockdiag, s[p] + ds[p], 0.0) * jnp.exp(c_end)

    @pl.when(c == nc - 1)
    def _():
        sfin_ref[...] = st_ref[...]


def wkv_scan(r, kk, v, lw, kd, ka, s0):
    b, NP, L, _ = r.shape
    T = WKV_T
    nc = L // T

    def cidx(d, c):
        return jnp.where(d == 0, c, nc - 1 - c)

    shared = lambda: pl.BlockSpec((None, NP, T, LANES), lambda d, bi, c: (bi, 0, cidx(d, c), 0))
    perdir = lambda: pl.BlockSpec((None, None, NP, T, LANES), lambda d, bi, c: (d, bi, 0, cidx(d, c), 0))
    state = lambda: pl.BlockSpec((None, None, NP, LANES, LANES), lambda d, bi, c: (d, bi, 0, 0, 0))
    return pl.pallas_call(
        _wkv_kernel,
        grid=(2, b, nc),
        in_specs=[shared(), shared(), shared(), perdir(), perdir(), perdir(), state()],
        out_specs=[perdir(), state()],
        out_shape=[jax.ShapeDtypeStruct((2, b, NP, L, LANES), F32),
                   jax.ShapeDtypeStruct((2, b, NP, LANES, LANES), F32)],
        scratch_shapes=[pltpu.VMEM((NP, LANES, LANES), F32)],
        compiler_params=_cparams(("parallel", "parallel", "arbitrary")),
        name="wkv_scan",
    )(r, kk, v, lw, kd, ka, s0)


def _odd_out_kernel(y0_ref, y1_ref, bo_ref, g_ref, lw_ref, lb_ref, w_ref, xres_ref, gate_ref,
                    lng_ref, lnb_ref, sc_ref, sh_ref, xo_ref, h2_ref, lhs_ref):
    li = lax.broadcasted_iota(jnp.int32, (LANES, LANES), 0) // RWKV_HEADSIZE
    lj = lax.broadcasted_iota(jnp.int32, (LANES, LANES), 1) // RWKV_HEADSIZE
    ones_blk = jnp.where(li == lj, 1.0, 0.0).astype(BF16)
    inv = 1.0 / RWKV_HEADSIZE
    for p in range(RWKV_PAIRS):
        sl = slice(p * LANES, (p + 1) * LANES)
        y = y0_ref[p] + y1_ref[p]
        mu = _seg_sum64(y, ones_blk) * inv
        dlt = y - mu
        var = _seg_sum64(dlt * dlt, ones_blk) * inv
        yn = dlt * lax.rsqrt(var + GN_EPS) * lw_ref[:, sl] + lb_ref[:, sl]
        lhs_ref[:, sl] = ((yn + bo_ref[p].astype(F32)) * g_ref[p].astype(F32)).astype(BF16)
    ymix = _dot(lhs_ref[...], w_ref[...])
    xn = _layer_norm(ALPHA * xres_ref[...] + gate_ref[...] * ymix, lng_ref[...], lnb_ref[...])
    xo_ref[...] = xn
    h2_ref[...] = xn * (1.0 + sc_ref[...]) + sh_ref[...]


def odd_out(y2, bonus, g, lnx_w, lnx_b, w_o, x_res, gate, ln_g, ln_b, sc2, sh2, tm):
    b, L, D = x_res.shape
    NP = RWKV_PAIRS
    vec = lambda: pl.BlockSpec((1, D), lambda bi, i: (0, 0))
    bvec = lambda: pl.BlockSpec((None, 1, D), lambda bi, i: (bi, 0, 0))
    pair = lambda: pl.BlockSpec((None, NP, tm, LANES), lambda bi, i: (bi, 0, i, 0))
    return pl.pallas_call(
        _odd_out_kernel,
        grid=(b, L // tm),
        in_specs=[pl.BlockSpec((None, None, NP, tm, LANES), lambda bi, i: (0, bi, 0, i, 0)),
                  pl.BlockSpec((None, None, NP, tm, LANES), lambda bi, i: (1, bi, 0, i, 0)),
                  pair(), pair(), vec(), vec(),
                  pl.BlockSpec((D, D), lambda bi, i: (0, 0)),
                  pl.BlockSpec((None, tm, D), lambda bi, i: (bi, i, 0)),
                  bvec(), vec(), vec(), bvec(), bvec()],
        out_specs=[pl.BlockSpec((None, tm, D), lambda bi, i: (bi, i, 0)),
                   pl.BlockSpec((None, tm, D), lambda bi, i: (bi, i, 0))],
        out_shape=[jax.ShapeDtypeStruct((b, L, D), F32), jax.ShapeDtypeStruct((b, L, D), F32)],
        scratch_shapes=[pltpu.VMEM((tm, D), BF16)],
        compiler_params=_cparams(("parallel", "parallel")),
        name="odd_out",
    )(y2, y2, bonus, g, lnx_w, lnx_b, w_o, x_res, gate, ln_g, ln_b, sc2, sh2)


def _pad_cols(w, n):
    return jnp.pad(w, ((0, 0), (0, n - w.shape[1])))


def even_layer(x_lat, x_ctx, ml, mc, p):
    bsz = x_lat.shape[0]
    o1, o2, o3 = SSD_WIDTH, SSD_WIDTH + SSD_CONV_DIM, SSD_WIDTH + SSD_CONV_DIM + 2 * SSD_HEADS
    w_in = p["w_in"]
    w_pad = jnp.concatenate([w_in[:, :o1], w_in[:, o1:o2], w_in[:, o3:],
                             _pad_cols(w_in[:, o2:o3], PR_N - PR_DT)], axis=1).astype(BF16)
    dt_bias128 = _pad_cols(p["dt_bias"].reshape(1, 2 * SSD_HEADS), LANES)
    a_log128 = _pad_cols(p["a_log"].reshape(1, 2 * SSD_HEADS), LANES)
    dskip = jnp.repeat(p["d_skip"], SSD_HEADDIM).reshape(1, SSD_WIDTH)
    norm_w = p["norm_w"].reshape(1, SSD_WIDTH)
    conv_b = p["conv_b"].reshape(1, SSD_CONV_DIM)
    w_out = p["w_out"].astype(BF16)
    cc, sc = _dft_tables(FNET_GW)
    cs = jnp.asarray(np.concatenate([cc, sc], axis=1), BF16)

    def run(x, m, s0):
        L = x.shape[1]
        tm = min(L, 512)
        pr = modmm(x, m[:, 1], m[:, 0], w_pad, tm, 768)
        xbc = conv_silu(pr, p["conv_w"], conv_b)
        y2, s_fin = ssd_scan(xbc, pr, dt_bias128, a_log128, s0)
        cl, sl = _dft_tables(L)
        wpos = jnp.asarray(np.concatenate([cl, -sl], axis=1), BF16)
        xcs = fnet_chan(pr, cs, tm).reshape(bsz, 2 * L, FNET_WIDTH)
        fmix = fnet_pos(wpos, xcs, min(L, 512), 512)
        x_new, h2 = even_out(y2, xbc, pr, fmix, dskip, norm_w, w_out, x, m[:, 2], p["ln_g"], p["ln_b"],
                             m[:, 4], m[:, 3], min(L, 256))
        return x_new, h2, s_fin

    s0 = jnp.zeros((2, bsz, SSD_GROUPS, SSD_STATE, SSD_HPG * SSD_HEADDIM), F32)
    xc_new, h2c, s_ctx = run(x_ctx, mc, s0)
    xl_new, h2l, _ = run(x_lat, ml, s_ctx)
    return xl_new, h2l, xc_new, h2c


def odd_layer(x_lat, x_ctx, ml, mc, p, need_ctx):
    bsz = x_lat.shape[0]
    D = D_MODEL
    mu = p["mu"]
    mu6 = jnp.stack([mu[0], mu[2], mu[3], mu[1], mu[4], mu[5]]).reshape(6, 1, D)
    w_rkv = p["w_rkv"]
    w1cat = _pad_cols(jnp.concatenate([p["w1"][0], p["w1"][1]], axis=1), LORA_PAD)
    a1cat = _pad_cols(jnp.concatenate([p["a1"][0], p["a1"][1]], axis=1), LORA_PAD)
    wcat = jnp.concatenate([w_rkv[0], w_rkv[1], w_rkv[2], w1cat, a1cat, p["g1"]], axis=1).astype(BF16)

    def lora2(w2):
        z = jnp.zeros((2, LORA_PAD, D), F32)
        z = z.at[0, 0:w2.shape[1]].set(w2[0])
        z = z.at[1, w2.shape[1]:2 * w2.shape[1]].set(w2[1])
        return z.astype(BF16)

    w2z = lora2(p["w2"])
    a2z = lora2(p["a2"])
    g2 = p["g2"].astype(BF16)
    w0 = p["w0"].reshape(2, 1, D)
    a0 = p["a0"].reshape(2, 1, D)
    k_k = p["k_k"].reshape(1, D)
    k_a = p["k_a"].reshape(1, D)
    r_k = p["r_k"].reshape(1, D)
    lnx_w = p["lnx_w"].reshape(1, D)
    lnx_b = p["lnx_b"].reshape(1, D)
    w_o = p["w_o"].astype(BF16)

    def run(x, m, s0, grid_w, need_out):
        L = x.shape[1]
        tm = min(L, 512)
        proj = rwkv_proj(x, m[:, 1], m[:, 0], mu6, wcat, tm, 256, grid_w)
        r, kk, v, bonus, g, lw, kd, ka = rwkv_post(proj, w2z, a2z, g2, w0, a0, k_k, k_a, r_k, min(L, 256))
        y2, s_fin = wkv_scan(r, kk, v, lw, kd, ka, s0)
        if not need_out:
            return None, None, s_fin
        x_new, h2 = odd_out(y2, bonus, g, lnx_w, lnx_b, w_o, x, m[:, 2], p["ln_g"], p["ln_b"],
                            m[:, 4], m[:, 3], min(L, 256))
        return x_new, h2, s_fin

    s0 = jnp.zeros((2, bsz, RWKV_PAIRS, LANES, LANES), F32)
    xc_new, h2c, s_ctx = run(x_ctx, mc, s0, None, need_ctx)
    xl_new, h2l, _ = run(x_lat, ml, s_ctx, 64, True)
    return xl_new, h2l, xc_new, h2c


def kernel(x, c, ctx, c_ctx, w_mod, b_mod, ln_g, ln_b, ssd_w_in, ssd_conv_w, ssd_conv_b, ssd_a_log, ssd_dt_bias, ssd_d, ssd_norm_w, even_w_out, rwkv_mu, rwkv_w_rkv, rwkv_w_o, rwkv_w0, rwkv_w1, rwkv_w2, rwkv_a0, rwkv_a1, rwkv_a2, rwkv_g1, rwkv_g2, rwkv_k_k, rwkv_k_a, rwkv_r_k, rwkv_lnx_w, rwkv_lnx_b, router_w, router_bias, moe_w_gate, moe_w_up, moe_w_down):
    bsz, L, D = x.shape
    Lc = ctx.shape[1]
    n_lat = bsz * L
    router_w128 = _pad_cols(router_w, LANES)
    router_b128 = jnp.pad(router_bias.reshape(1, N_EXPERTS), ((0, 0), (0, LANES - N_EXPERTS)),
                          constant_values=-jnp.inf)
    cc = jnp.concatenate([c, c_ctx[None, :], jnp.zeros((7, D), F32)], axis=0)
    x_lat, x_ctx = x, ctx
    for i in range(DEPTH):
        last = i == DEPTH - 1
        j = i // 2
        m_all = dense_silu(cc, w_mod[i], b_mod[i].reshape(1, 6 * D))
        ml = m_all[:bsz].reshape(bsz, 6, 1, D)
        mc = jnp.broadcast_to(m_all[bsz].reshape(1, 6, 1, D), (bsz, 6, 1, D))
        lng0, lnb0 = ln_g[i, 0].reshape(1, D), ln_b[i, 0].reshape(1, D)
        lng1, lnb1 = ln_g[i, 1].reshape(1, D), ln_b[i, 1].reshape(1, D)
        if i % 2 == 0:
            p = dict(w_in=ssd_w_in[j], conv_w=ssd_conv_w[j], conv_b=ssd_conv_b[j], a_log=ssd_a_log[j],
                     dt_bias=ssd_dt_bias[j], d_skip=ssd_d[j], norm_w=ssd_norm_w[j], w_out=even_w_out[j],
                     ln_g=lng0, ln_b=lnb0)
            xl, h2l, xc, h2c = even_layer(x_lat, x_ctx, ml, mc, p)
        else:
            p = dict(mu=rwkv_mu[j], w_rkv=rwkv_w_rkv[j], w_o=rwkv_w_o[j], w0=rwkv_w0[j], w1=rwkv_w1[j],
                     w2=rwkv_w2[j], a0=rwkv_a0[j], a1=rwkv_a1[j], a2=rwkv_a2[j], g1=rwkv_g1[j],
                     g2=rwkv_g2[j], k_k=rwkv_k_k[j], k_a=rwkv_k_a[j], r_k=rwkv_r_k[j],
                     lnx_w=rwkv_lnx_w[j], lnx_b=rwkv_lnx_b[j], ln_g=lng0, ln_b=lnb0)
            xl, h2l, xc, h2c = odd_layer(x_lat, x_ctx, ml, mc, p, not last)
        tm = 256
        out = moe_layer(h2l.reshape(n_lat, D), xl.reshape(n_lat, D), ml[:, 5],
                        lambda t: t // (L // tm), router_w128, router_b128,
                        moe_w_gate[i], moe_w_up[i], moe_w_down[i], lng1, lnb1, tm)
        x_lat = out.reshape(bsz, L, D)
        if not last:
            out_c = moe_layer(h2c.reshape(bsz * Lc, D), xc.reshape(bsz * Lc, D), mc[:, 5],
                              lambda t: t // (Lc // tm), router_w128, router_b128,
                              moe_w_gate[i], moe_w_up[i], moe_w_down[i], lng1, lnb1, tm)
            x_ctx = out_c.reshape(bsz, Lc, D)
    return x_lat
```

```python
import functools
import math

import numpy as np
import jax
import jax.numpy as jnp
from jax import lax
from jax.experimental import pallas as pl
from jax.experimental.pallas import tpu as pltpu

F32 = jnp.float32
BF16 = jnp.bfloat16

V7X_VMEM_BYTES = 64 * 1024 * 1024
VMEM_LIMIT = V7X_VMEM_BYTES - 8 * 1024 * 1024
LANES = 128

D_MODEL = 2048
DEPTH = 2
ALPHA = (2 * DEPTH) ** 0.25
LN_EPS = 1e-5
GN_EPS = 64e-5

SSD_WIDTH = 1024
SSD_HEADDIM = 64
SSD_HEADS = 16
SSD_GROUPS = 2
SSD_HPG = 8
SSD_STATE = 128
SSD_CONV_DIM = SSD_WIDTH + 2 * SSD_GROUPS * SSD_STATE
FNET_WIDTH = 1024
FNET_GROUPS = 4
FNET_GW = 256
SSD_Q = 128
PR_Z, PR_XBC, PR_F, PR_DT, PR_N = 0, 1024, 2560, 3584, 3840

RWKV_HEADSIZE = 64
RWKV_HEADS = 32
RWKV_PAIRS = 16
DECAY_LORA = 96
AAA_LORA = 96
GATE_LORA = 256
LORA_PAD = 256
RW_N = 3 * D_MODEL + 3 * LORA_PAD
WKV_T = 64

N_EXPERTS = 32
N_GROUPS = 4
EXPERTS_PER_GROUP = 8
TOP_K = 2
D_EXPERT = 512
MOE_BM = 256


def _cparams(sem):
    return pltpu.CompilerParams(dimension_semantics=sem, vmem_limit_bytes=VMEM_LIMIT)


def _dot(a, b):
    return jnp.dot(a, b, preferred_element_type=F32)


def _dot_nt(a, b):
    return lax.dot_general(a, b, (((1,), (1,)), ((), ())), preferred_element_type=F32)


def _dot_tn(a, b):
    return lax.dot_general(a, b, (((0,), (0,)), ((), ())), preferred_element_type=F32)


def _split2(a):
    hi = a.astype(BF16)
    lo = (a - hi.astype(F32)).astype(BF16)
    return hi, lo


def _split3(a):
    hi = a.astype(BF16)
    r1 = a - hi.astype(F32)
    mid = r1.astype(BF16)
    lo = (r1 - mid.astype(F32)).astype(BF16)
    return hi, mid, lo


def _dot_exact_lhs(a_bf16, b_f32):
    b1, b2, b3 = _split3(b_f32)
    return _dot(a_bf16, b1) + _dot(a_bf16, b2) + _dot(a_bf16, b3)


def _dot_exact_rhs(a_f32, b_bf16):
    a1, a2, a3 = _split3(a_f32)
    return _dot(a1, b_bf16) + _dot(a2, b_bf16) + _dot(a3, b_bf16)


def _dot_x3(a_f32, b_f32):
    a1, a2 = _split2(a_f32)
    b1, b2 = _split2(b_f32)
    return _dot(a1, b1) + _dot(a1, b2) + _dot(a2, b1)


def _silu(x):
    return x * jax.nn.sigmoid(x)


def _softplus(x):
    return jnp.maximum(x, 0.0) + jnp.log(1.0 + jnp.exp(-jnp.abs(x)))


def _layer_norm(v, g, b):
    mu = jnp.mean(v, axis=-1, keepdims=True)
    d = v - mu
    var = jnp.mean(d * d, axis=-1, keepdims=True)
    return d * lax.rsqrt(var + LN_EPS) * g + b


def _dense_silu_kernel(x_ref, w_ref, b_ref, o_ref):
    x = _silu(x_ref[...]).astype(BF16)
    o_ref[...] = _dot(x, w_ref[...].astype(BF16)) + b_ref[...]


def dense_silu(x, w, layer, bias, tn=512):
    m, k = x.shape
    n = w.shape[2]
    return pl.pallas_call(
        _dense_silu_kernel,
        grid=(n // tn,),
        in_specs=[pl.BlockSpec((m, k), lambda j: (0, 0)),
                  pl.BlockSpec((None, k, tn), lambda j: (layer, 0, j)),
                  pl.BlockSpec((1, tn), lambda j: (0, j))],
        out_specs=pl.BlockSpec((m, tn), lambda j: (0, j)),
        out_shape=jax.ShapeDtypeStruct((m, n), F32),
        compiler_params=_cparams(("parallel",)),
        name="dense_silu",
    )(x, w, bias)


def _modmm_kernel(x_ref, sc_ref, sh_ref, w_ref, o_ref, h_ref):
    @pl.when(pl.program_id(2) == 0)
    def _():
        h_ref[...] = (x_ref[...] * (1.0 + sc_ref[...]) + sh_ref[...]).astype(BF16)

    o_ref[...] = _dot(h_ref[...], w_ref[...]).astype(o_ref.dtype)


def modmm(x, sc, sh, w, tm, tn, out_dtype=F32):
    b, L, k = x.shape
    n = w.shape[1]
    return pl.pallas_call(
        _modmm_kernel,
        grid=(b, L // tm, n // tn),
        in_specs=[pl.BlockSpec((None, tm, k), lambda bi, i, j: (bi, i, 0)),
                  pl.BlockSpec((None, 1, k), lambda bi, i, j: (bi, 0, 0)),
                  pl.BlockSpec((None, 1, k), lambda bi, i, j: (bi, 0, 0)),
                  pl.BlockSpec((k, tn), lambda bi, i, j: (0, j))],
        out_specs=pl.BlockSpec((None, tm, tn), lambda bi, i, j: (bi, i, j)),
        out_shape=jax.ShapeDtypeStruct((b, L, n), out_dtype),
        scratch_shapes=[pltpu.VMEM((tm, k), BF16)],
        compiler_params=_cparams(("parallel", "parallel", "arbitrary")),
        name="modmm",
    )(x, sc, sh, w)


def _conv_silu_kernel(x_ref, w_ref, b_ref, o_ref):
    x = x_ref[...]
    L = x.shape[0]
    row = lax.broadcasted_iota(jnp.int32, x.shape, 0)
    prev = jnp.where(row == 0, 0.0, pltpu.roll(x, 1, axis=0))
    nxt = jnp.where(row == L - 1, 0.0, pltpu.roll(x, L - 1, axis=0))
    w = w_ref[...]
    y = prev * w[0:1] + x * w[1:2] + nxt * w[2:3] + b_ref[...]
    o_ref[...] = _silu(y)


def conv_silu(pr, conv_w, conv_b, tc=512):
    b, L, _ = pr.shape
    off = PR_XBC // tc
    return pl.pallas_call(
        _conv_silu_kernel,
        grid=(b, SSD_CONV_DIM // tc),
        in_specs=[pl.BlockSpec((None, L, tc), lambda bi, j: (bi, 0, off + j)),
                  pl.BlockSpec((3, tc), lambda bi, j: (0, j)),
                  pl.BlockSpec((1, tc), lambda bi, j: (0, j))],
        out_specs=pl.BlockSpec((None, L, tc), lambda bi, j: (bi, 0, j)),
        out_shape=jax.ShapeDtypeStruct((b, L, SSD_CONV_DIM), F32),
        compiler_params=_cparams(("parallel", "parallel")),
        name="conv_silu",
    )(pr, conv_w, conv_b)


def _ssd_kernel(x_ref, bc_ref, dtr_ref, dtb_ref, alog_ref, s0_ref, y_ref, sfin_ref, st_ref):
    Q = SSD_Q
    d = pl.program_id(0)
    c = pl.program_id(2)
    nc = pl.num_programs(2)
    fwd = d == 0

    @pl.when(c == 0)
    def _():
        st_ref[...] = s0_ref[...]

    row = lax.broadcasted_iota(jnp.int32, (Q, Q), 0)
    col = lax.broadcasted_iota(jnp.int32, (Q, Q), 1)
    sgn = 1 - 2 * d
    incl = (row - col) * sgn >= 0
    tri = jnp.where(incl, 1.0, 0.0).astype(BF16)

    dt_all = _softplus(dtr_ref[:, 0:LANES] + dtb_ref[...])
    dA_all = dt_all * (-jnp.exp(alog_ref[...]))
    acum_all = _dot_exact_lhs(tri, dA_all)
    acum_t = acum_all.T

    lane_h = lax.broadcasted_iota(jnp.int32, (LANES, SSD_HPG * SSD_HEADDIM), 0)
    col_h = lax.broadcasted_iota(jnp.int32, (LANES, SSD_HPG * SSD_HEADDIM), 1) // SSD_HEADDIM
    lane128 = lax.broadcasted_iota(jnp.int32, (Q, LANES), 1)
    m_lo = lane128 < SSD_HEADDIM

    for g in range(SSD_GROUPS):
        expand = jnp.where(lane_h == d * SSD_HEADS + g * SSD_HPG + col_h, 1.0, 0.0).astype(BF16)
        dt_e = _dot_exact_rhs(dt_all, expand)
        a_e = _dot_exact_rhs(acum_all, expand)
        a_tot = jnp.where(fwd, a_e[Q - 1:Q], a_e[0:1])
        xg = x_ref[:, g * 512:(g + 1) * 512]
        xdt = xg * dt_e
        bg = bc_ref[:, g * SSD_STATE:(g + 1) * SSD_STATE]
        cg = bc_ref[:, SSD_GROUPS * SSD_STATE + g * SSD_STATE:SSD_GROUPS * SSD_STATE + (g + 1) * SSD_STATE]
        bgb = bg.astype(BF16)
        cgb = cg.astype(BF16)
        scores = _dot_nt(cgb, bgb)
        st = st_ref[g]
        y_off = _dot(cgb, st.astype(BF16)) * jnp.exp(a_e)
        xdt_b = xdt.astype(BF16)
        for pr_i in range(SSD_HPG // 2):
            xp = xdt_b[:, pr_i * LANES:(pr_i + 1) * LANES]
            acc = y_off[:, pr_i * LANES:(pr_i + 1) * LANES]
            for half in range(2):
                r = 2 * pr_i + half
                hrow = g * SSD_HPG + r
                a_col = jnp.where(fwd, acum_all[:, hrow:hrow + 1],
                                  acum_all[:, SSD_HEADS + hrow:SSD_HEADS + hrow + 1])
                a_row = jnp.where(fwd, acum_t[hrow:hrow + 1, :],
                                  acum_t[SSD_HEADS + hrow:SSD_HEADS + hrow + 1, :])
                seg = jnp.minimum(a_col - a_row, 0.0)
                m = jnp.where(incl, scores * jnp.exp(seg), 0.0).astype(BF16)
                xh = jnp.where(m_lo if half == 0 else jnp.logical_not(m_lo), xp, jnp.zeros_like(xp))
                acc = acc + _dot(m, xh)
            y_ref[:, g * 512 + pr_i * LANES:g * 512 + (pr_i + 1) * LANES] = acc
        xde = (xdt * jnp.exp(a_tot - a_e)).astype(BF16)
        st_ref[g] = st * jnp.exp(a_tot) + _dot_tn(bgb, xde)

    @pl.when(c == nc - 1)
    def _():
        sfin_ref[...] = st_ref[...]


def ssd_scan(xbc, pr, dt_bias128, a_log128, s0):
    b, L, _ = xbc.shape
    Q = SSD_Q
    nc = L // Q

    def cidx(d, c):
        return jnp.where(d == 0, c, nc - 1 - c)

    return pl.pallas_call(
        _ssd_kernel,
        grid=(2, b, nc),
        in_specs=[pl.BlockSpec((None, Q, SSD_WIDTH), lambda d, bi, c: (bi, cidx(d, c), 0)),
                  pl.BlockSpec((None, Q, 512), lambda d, bi, c: (bi, cidx(d, c), SSD_WIDTH // 512)),
                  pl.BlockSpec((None, Q, 256), lambda d, bi, c: (bi, cidx(d, c), PR_DT // 256)),
                  pl.BlockSpec((1, LANES), lambda d, bi, c: (0, 0)),
                  pl.BlockSpec((1, LANES), lambda d, bi, c: (0, 0)),
                  pl.BlockSpec((None, None, SSD_GROUPS, SSD_STATE, 512), lambda d, bi, c: (d, bi, 0, 0, 0))],
        out_specs=[pl.BlockSpec((None, None, Q, SSD_WIDTH), lambda d, bi, c: (d, bi, cidx(d, c), 0)),
                   pl.BlockSpec((None, None, SSD_GROUPS, SSD_STATE, 512), lambda d, bi, c: (d, bi, 0, 0, 0))],
        out_shape=[jax.ShapeDtypeStruct((2, b, L, SSD_WIDTH), F32),
                   jax.ShapeDtypeStruct((2, b, SSD_GROUPS, SSD_STATE, 512), F32)],
        scratch_shapes=[pltpu.VMEM((SSD_GROUPS, SSD_STATE, 512), F32)],
        compiler_params=_cparams(("parallel", "parallel", "arbitrary")),
        name="ssd_scan",
    )(xbc, xbc, pr, dt_bias128, a_log128, s0)


def _fnet_chan_kernel(f_ref, cs_ref, o_ref):
    res = _dot(f_ref[...].astype(BF16), cs_ref[...])
    o_ref[0] = res[:, :FNET_GW].astype(o_ref.dtype)
    o_ref[1] = res[:, FNET_GW:].astype(o_ref.dtype)


def fnet_chan(pr, cs, tm):
    b, L, _ = pr.shape
    off = PR_F // FNET_GW
    return pl.pallas_call(
        _fnet_chan_kernel,
        grid=(b, L // tm, FNET_GROUPS),
        in_specs=[pl.BlockSpec((None, tm, FNET_GW), lambda bi, i, g: (bi, i, off + g)),
                  pl.BlockSpec((FNET_GW, 2 * FNET_GW), lambda bi, i, g: (0, 0))],
        out_specs=pl.BlockSpec((None, 2, tm, FNET_GW), lambda bi, i, g: (bi, 0, i, g)),
        out_shape=jax.ShapeDtypeStruct((b, 2, L, FNET_WIDTH), BF16),
        compiler_params=_cparams(("parallel", "parallel", "parallel")),
        name="fnet_chan",
    )(pr, cs)


def _fnet_pos_kernel(w_ref, x_ref, o_ref):
    o_ref[...] = _dot(w_ref[...], x_ref[...])


def fnet_pos(wpos, xcs, tm, tn):
    b, k, n = xcs.shape
    L = wpos.shape[0]
    return pl.pallas_call(
        _fnet_pos_kernel,
        grid=(b, n // tn, L // tm),
        in_specs=[pl.BlockSpec((tm, k), lambda bi, j, i: (i, 0)),
                  pl.BlockSpec((None, k, tn), lambda bi, j, i: (bi, 0, j))],
        out_specs=pl.BlockSpec((None, tm, tn), lambda bi, j, i: (bi, i, j)),
        out_shape=jax.ShapeDtypeStruct((b, L, n), F32),
        compiler_params=_cparams(("parallel", "parallel", "parallel")),
        name="fnet_pos",
    )(wpos, xcs)


def _dft_tables(n):
    k = np.arange(n, dtype=np.int64)
    ang = 2.0 * np.pi * ((k[:, None] * k[None, :]) % n).astype(np.float64) / n
    s = 1.0 / math.sqrt(n)
    return np.cos(ang) * s, np.sin(ang) * s


def _even_out_kernel(y0_ref, y1_ref, xs_ref, z_ref, f_ref, dsk_ref, nw_ref, w_ref, xres_ref, gate_ref,
                     lng_ref, lnb_ref, sc_ref, sh_ref, xo_ref, h2_ref, lhs_ref):
    y = y0_ref[...] + y1_ref[...] + dsk_ref[...] * xs_ref[...]
    u = y * _silu(z_ref[...])
    gw = SSD_WIDTH // SSD_GROUPS
    for g in range(SSD_GROUPS):
        ug = u[:, g * gw:(g + 1) * gw]
        ms = jnp.mean(ug * ug, axis=-1, keepdims=True)
        lhs_ref[:, g * gw:(g + 1) * gw] = (ug * lax.rsqrt(ms + LN_EPS) * nw_ref[:, g * gw:(g + 1) * gw]).astype(BF16)
    lhs_ref[:, SSD_WIDTH:] = f_ref[...].astype(BF16)
    ymix = _dot(lhs_ref[...], w_ref[...])
    xn = _layer_norm(ALPHA * xres_ref[...] + gate_ref[...] * ymix, lng_ref[...], lnb_ref[...])
    xo_ref[...] = xn
    h2_ref[...] = xn * (1.0 + sc_ref[...]) + sh_ref[...]


def even_out(y2, xbc, pr, fmix, dskip, norm_w, w_out, x_res, gate, ln_g, ln_b, sc2, sh2, tm):
    b, L, D = x_res.shape
    vec = lambda: pl.BlockSpec((1, D), lambda bi, i: (0, 0))
    bvec = lambda: pl.BlockSpec((None, 1, D), lambda bi, i: (bi, 0, 0))
    return pl.pallas_call(
        _even_out_kernel,
        grid=(b, L // tm),
        in_specs=[pl.BlockSpec((None, None, tm, SSD_WIDTH), lambda bi, i: (0, bi, i, 0)),
                  pl.BlockSpec((None, None, tm, SSD_WIDTH), lambda bi, i: (1, bi, i, 0)),
                  pl.BlockSpec((None, tm, SSD_WIDTH), lambda bi, i: (bi, i, 0)),
                  pl.BlockSpec((None, tm, SSD_WIDTH), lambda bi, i: (bi, i, PR_Z // SSD_WIDTH)),
                  pl.BlockSpec((None, tm, FNET_WIDTH), lambda bi, i: (bi, i, 0)),
                  pl.BlockSpec((1, SSD_WIDTH), lambda bi, i: (0, 0)),
                  pl.BlockSpec((1, SSD_WIDTH), lambda bi, i: (0, 0)),
                  pl.BlockSpec((D, D), lambda bi, i: (0, 0)),
                  pl.BlockSpec((None, tm, D), lambda bi, i: (bi, i, 0)),
                  bvec(), vec(), vec(), bvec(), bvec()],
        out_specs=[pl.BlockSpec((None, tm, D), lambda bi, i: (bi, i, 0)),
                   pl.BlockSpec((None, tm, D), lambda bi, i: (bi, i, 0))],
        out_shape=[jax.ShapeDtypeStruct((b, L, D), F32), jax.ShapeDtypeStruct((b, L, D), F32)],
        scratch_shapes=[pltpu.VMEM((tm, D), BF16)],
        compiler_params=_cparams(("parallel", "parallel")),
        name="even_out",
    )(y2, y2, xbc, pr, fmix, dskip, norm_w, w_out, x_res, gate, ln_g, ln_b, sc2, sh2)


def _router_kernel(h_ref, w_ref, bias_ref, o_ref):
    logits = _dot_x3(h_ref[...], w_ref[...])
    s = jax.nn.sigmoid(logits)
    lane = lax.broadcasted_iota(jnp.int32, s.shape, 1)
    neg = jnp.float32(-jnp.inf)
    ssel = s + bias_ref[...]
    big = jnp.int32(4 * LANES)

    def top2(mask):
        v = jnp.where(mask, ssel, neg)
        m1 = jnp.max(v, axis=-1, keepdims=True)
        i1 = jnp.min(jnp.where(v == m1, lane, big), axis=-1, keepdims=True)
        v2 = jnp.where(lane == i1, neg, v)
        m2 = jnp.max(v2, axis=-1, keepdims=True)
        i2 = jnp.min(jnp.where(v2 == m2, lane, big), axis=-1, keepdims=True)
        return m1 + m2, i1, i2

    best, bi1, bi2 = top2(lane < EXPERTS_PER_GROUP)
    for g in range(1, N_GROUPS):
        sc, i1, i2 = top2((lane >= g * EXPERTS_PER_GROUP) & (lane < (g + 1) * EXPERTS_PER_GROUP))
        better = sc > best
        best = jnp.where(better, sc, best)
        bi1 = jnp.where(better, i1, bi1)
        bi2 = jnp.where(better, i2, bi2)
    s1 = jnp.sum(jnp.where(lane == bi1, s, 0.0), axis=-1, keepdims=True)
    s2 = jnp.sum(jnp.where(lane == bi2, s, 0.0), axis=-1, keepdims=True)
    tot = s1 + s2
    out = jnp.where(lane == 0, bi1.astype(F32),
                    jnp.where(lane == 1, bi2.astype(F32),
                              jnp.where(lane == 2, s1 / tot, jnp.where(lane == 3, s2 / tot, 0.0))))
    o_ref[...] = out


def router(h, w128, bias128, tm):
    T, D = h.shape
    return pl.pallas_call(
        _router_kernel,
        grid=(T // tm,),
        in_specs=[pl.BlockSpec((tm, D), lambda i: (i, 0)),
                  pl.BlockSpec((D, LANES), lambda i: (0, 0)),
                  pl.BlockSpec((1, LANES), lambda i: (0, 0))],
        out_specs=pl.BlockSpec((tm, LANES), lambda i: (i, 0)),
        out_shape=jax.ShapeDtypeStruct((T, LANES), F32),
        compiler_params=_cparams(("parallel",)),
        name="router",
    )(h, w128, bias128)


def _moe_kernel(be_ref, nused_ref, src_ref, nxt_ref, h_hbm, wg_ref, wu_ref, wd_ref, o_ref,
                xb0, xb1, wgb, wub, wdb, sem):
    i = pl.program_id(0)
    nu = nused_ref[0]
    bm = xb0.shape[0]

    def issue(idx_ref, dst, s):
        for r in range(bm):
            pltpu.make_async_copy(h_hbm.at[pl.ds(idx_ref[0, 0, r], 1)], dst.at[pl.ds(r, 1)], sem.at[s]).start()

    def wait(dst, s):
        pltpu.make_async_copy(h_hbm.at[pl.ds(0, bm)], dst, sem.at[s]).wait()

    @pl.when(i == 0)
    def _():
        issue(src_ref, xb0, 0)

    @pl.when(i < nu)
    def _():
        prev = be_ref[jnp.maximum(i - 1, 0)]

        @pl.when((i == 0) | (be_ref[i] != prev))
        def _():
            wgb[...] = wg_ref[...].astype(BF16)
            wub[...] = wu_ref[...].astype(BF16)
            wdb[...] = wd_ref[...].astype(BF16)

        for par in range(2):
            cur, nxt = (xb0, xb1) if par == 0 else (xb1, xb0)

            @pl.when(i % 2 == par)
            def _(cur=cur, nxt=nxt, par=par):
                wait(cur, par)
                issue(nxt_ref, nxt, 1 - par)
                x = cur[...].astype(BF16)
                hid = _silu(_dot(x, wgb[...])) * _dot(x, wub[...])
                o_ref[...] = _dot(hid.astype(BF16), wdb[...])

                @pl.when(i == nu - 1)
                def _():
                    wait(nxt, 1 - par)

    @pl.when(i >= nu)
    def _():
        o_ref[...] = jnp.zeros_like(o_ref)


def moe_experts(h, block_e, nused, src_tok, w_gate, w_up, w_down, layer):
    T, D = h.shape
    nblk = block_e.shape[0]
    bm = MOE_BM
    last = lambda i, nu: jnp.minimum(i, nu[0] - 1)
    gs = pltpu.PrefetchScalarGridSpec(
        num_scalar_prefetch=2,
        grid=(nblk,),
        in_specs=[pl.BlockSpec((1, 1, bm), lambda i, be, nu: (last(i, nu), 0, 0), memory_space=pltpu.SMEM),
                  pl.BlockSpec((1, 1, bm), lambda i, be, nu: (last(i + 1, nu), 0, 0), memory_space=pltpu.SMEM),
                  pl.BlockSpec(memory_space=pl.ANY),
                  pl.BlockSpec((None, None, D, D_EXPERT), lambda i, be, nu: (layer, be[last(i, nu)], 0, 0)),
                  pl.BlockSpec((None, None, D, D_EXPERT), lambda i, be, nu: (layer, be[last(i, nu)], 0, 0)),
                  pl.BlockSpec((None, None, D_EXPERT, D), lambda i, be, nu: (layer, be[last(i, nu)], 0, 0))],
        out_specs=pl.BlockSpec((bm, D), lambda i, be, nu: (i, 0)),
        scratch_shapes=[pltpu.VMEM((bm, D), F32),
                        pltpu.VMEM((bm, D), F32),
                        pltpu.VMEM((D, D_EXPERT), BF16),
                        pltpu.VMEM((D, D_EXPERT), BF16),
                        pltpu.VMEM((D_EXPERT, D), BF16),
                        pltpu.SemaphoreType.DMA((2,))],
    )
    src3 = src_tok.reshape(nblk, 1, bm)
    return pl.pallas_call(
        _moe_kernel,
        grid_spec=gs,
        out_shape=jax.ShapeDtypeStruct((nblk * bm, D), F32),
        compiler_params=_cparams(("arbitrary",)),
        name="moe_experts",
    )(block_e, nused, src3, src3, h, w_gate, w_up, w_down)


def _moe_combine_kernel(dest_ref, nxt_ref, yb_hbm, rw_ref, x_ref, gate_ref, lng_ref, lnb_ref, o_ref, yb0, yb1, sem):
    i = pl.program_id(0)
    n = pl.num_programs(0)
    tm = x_ref.shape[0]

    def issue(idx_ref, dst, s):
        for r in range(tm):
            pltpu.make_async_copy(yb_hbm.at[pl.ds(idx_ref[0, 0, r], 1)], dst.at[0, pl.ds(r, 1)],
                                  sem.at[s]).start(priority=0)
            pltpu.make_async_copy(yb_hbm.at[pl.ds(idx_ref[0, 0, tm + r], 1)], dst.at[1, pl.ds(r, 1)],
                                  sem.at[s]).start(priority=1)

    def wait(dst, s):
        pltpu.make_async_copy(yb_hbm.at[pl.ds(0, tm)], dst.at[0], sem.at[s]).wait()
        pltpu.make_async_copy(yb_hbm.at[pl.ds(0, tm)], dst.at[1], sem.at[s]).wait()

    @pl.when(i == 0)
    def _():
        issue(dest_ref, yb0, 0)

    for par in range(2):
        cur, nxt = (yb0, yb1) if par == 0 else (yb1, yb0)

        @pl.when(i % 2 == par)
        def _(cur=cur, nxt=nxt, par=par):
            wait(cur, par)
            issue(nxt_ref, nxt, 1 - par)
            rw = rw_ref[...]
            y2 = rw[:, 2:3] * cur[0] + rw[:, 3:4] * cur[1]
            o_ref[...] = _layer_norm(ALPHA * x_ref[...] + gate_ref[...] * y2, lng_ref[...], lnb_ref[...])

            @pl.when(i == n - 1)
            def _():
                wait(nxt, 1 - par)


def moe_combine(dest_tiles, ybuf, rw, x, gates, gate_tile_map, ln_g, ln_b, tm):
    T, D = x.shape
    nt = T // tm
    return pl.pallas_call(
        _moe_combine_kernel,
        grid=(nt,),
        in_specs=[pl.BlockSpec((1, 1, 2 * tm), lambda i: (i, 0, 0), memory_space=pltpu.SMEM),
                  pl.BlockSpec((1, 1, 2 * tm), lambda i: (jnp.minimum(i + 1, nt - 1), 0, 0), memory_space=pltpu.SMEM),
                  pl.BlockSpec(memory_space=pl.ANY),
                  pl.BlockSpec((tm, LANES), lambda i: (i, 0)),
                  pl.BlockSpec((tm, D), lambda i: (i, 0)),
                  pl.BlockSpec((None, 1, D), lambda i: (gate_tile_map(i), 0, 0)),
                  pl.BlockSpec((1, D), lambda i: (0, 0)),
                  pl.BlockSpec((1, D), lambda i: (0, 0))],
        out_specs=pl.BlockSpec((tm, D), lambda i: (i, 0)),
        out_shape=jax.ShapeDtypeStruct((T, D), F32),
        scratch_shapes=[pltpu.VMEM((2, tm, D), F32), pltpu.VMEM((2, tm, D), F32), pltpu.SemaphoreType.DMA((2,))],
        compiler_params=_cparams(("arbitrary",)),
        name="moe_combine",
    )(dest_tiles, dest_tiles, ybuf, rw, x, gates, ln_g, ln_b)


def moe_layer(h2, x_res, gates, gate_tile_map, router_w128, router_b128, w_gate, w_up, w_down, layer, ln_g, ln_b,
              tm=256):
    T, D = h2.shape
    bm = MOE_BM
    rw = router(h2, router_w128, router_b128, tm)
    idx = rw[:, :TOP_K].astype(jnp.int32)
    e_flat = idx.reshape(-1)
    A = e_flat.shape[0]
    onehot = (e_flat[:, None] == jnp.arange(N_EXPERTS, dtype=jnp.int32)[None, :]).astype(jnp.int32)
    csum = jnp.cumsum(onehot, axis=0)
    counts = csum[-1]
    rank = jnp.sum(onehot * csum, axis=1) - 1
    padded = (counts + bm - 1) // bm * bm
    ends = jnp.cumsum(padded)
    pstart = ends - padded
    dest = pstart[e_flat] + rank
    nblk = (A + N_EXPERTS * (bm - 1) + bm - 1) // bm
    P = nblk * bm
    tok = jnp.arange(A, dtype=jnp.int32) // TOP_K
    src_tok = jnp.zeros((P,), jnp.int32).at[dest].set(tok, unique_indices=True, mode="promise_in_bounds")
    blk_start = jnp.arange(nblk, dtype=jnp.int32) * bm
    block_e = jnp.minimum(jnp.sum((ends[None, :] <= blk_start[:, None]).astype(jnp.int32), axis=1),
                          N_EXPERTS - 1).astype(jnp.int32)
    nused = (ends[-1] // bm).astype(jnp.int32).reshape(1)
    ybuf = moe_experts(h2, block_e, nused, src_tok, w_gate, w_up, w_down, layer)
    dest2 = dest.reshape(T // tm, tm, TOP_K)
    dest_tiles = jnp.concatenate([dest2[:, :, 0], dest2[:, :, 1]], axis=1).reshape(T // tm, 1, 2 * tm)
    return moe_combine(dest_tiles.astype(jnp.int32), ybuf, rw, x_res, gates, gate_tile_map, ln_g, ln_b, tm)


def _rwkv_proj_kernel(x_ref, xp_ref, xn_ref, sc_ref, sh_ref, mu_ref, w_ref, o_ref, h_ref, xx_ref, mix_ref,
                      *, grid_w, n_main):
    i = pl.program_id(1)
    j = pl.program_id(2)
    nt = pl.num_programs(1)
    tm, D = h_ref.shape
    q = D // 4

    @pl.when(j == 0)
    def _():
        sc = 1.0 + sc_ref[...]
        sh = sh_ref[...]
        h = x_ref[...] * sc + sh
        h_ref[...] = h
        row = lax.broadcasted_iota(jnp.int32, (tm, q), 0)
        if grid_w is None:
            prev = lambda a: jnp.where(row == 0, 0.0, pltpu.roll(a, 1, axis=0))
            nxt = lambda a: jnp.where(row == tm - 1, 0.0, pltpu.roll(a, tm - 1, axis=0))
            parts = [prev(h[:, 0:q]), nxt(h[:, q:2 * q]), prev(h[:, 2 * q:3 * q]), nxt(h[:, 3 * q:])]
        else:
            wpos = row % grid_w
            left = jnp.where(wpos == 0, 0.0, pltpu.roll(h[:, 0:q], 1, axis=0))
            right = jnp.where(wpos == grid_w - 1, 0.0, pltpu.roll(h[:, q:2 * q], tm - 1, axis=0))
            hp = xp_ref[:, 2 * q:3 * q] * sc[:, 2 * q:3 * q] + sh[:, 2 * q:3 * q]
            hn = xn_ref[:, 3 * q:] * sc[:, 3 * q:] + sh[:, 3 * q:]
            hp = jnp.where(i == 0, 0.0, hp)
            hn = jnp.where(i == nt - 1, 0.0, hn)
            up = jnp.concatenate([hp, h[:tm - grid_w, 2 * q:3 * q]], axis=0)
            down = jnp.concatenate([h[grid_w:, 3 * q:], hn], axis=0)
            parts = [left, right, up, down]
        for k in range(4):
            xx_ref[:, k * q:(k + 1) * q] = parts[k] - h[:, k * q:(k + 1) * q]

    is_main = j < 3 * n_main
    new_stream = jnp.where(is_main, j % n_main == 0, True)

    @pl.when(new_stream)
    def _():
        mix_ref[...] = (h_ref[...] + xx_ref[...] * mu_ref[...]).astype(BF16)

    o_ref[...] = _dot(mix_ref[...], w_ref[...]).astype(o_ref.dtype)


def rwkv_proj(x, sc, sh, mu6, wcat, tm, tn, grid_w):
    b, L, D = x.shape
    n = wcat.shape[1]
    n_main = D // tn
    nj = n // tn
    gw = 64 if grid_w is None else grid_w
    nh = L // gw

    def stream(j):
        return jnp.where(j < 3 * n_main, j // n_main, 3 + (j - 3 * n_main) // (LORA_PAD // tn))

    kern = functools.partial(_rwkv_proj_kernel, grid_w=grid_w, n_main=n_main)
    return pl.pallas_call(
        kern,
        grid=(b, L // tm, nj),
        in_specs=[pl.BlockSpec((None, tm, D), lambda bi, i, j: (bi, i, 0)),
                  pl.BlockSpec((None, gw, D), lambda bi, i, j: (bi, jnp.maximum(i * (tm // gw) - 1, 0), 0)),
                  pl.BlockSpec((None, gw, D), lambda bi, i, j: (bi, jnp.minimum((i + 1) * (tm // gw), nh - 1), 0)),
                  pl.BlockSpec((None, 1, D), lambda bi, i, j: (bi, 0, 0)),
                  pl.BlockSpec((None, 1, D), lambda bi, i, j: (bi, 0, 0)),
                  pl.BlockSpec((None, 1, D), lambda bi, i, j: (stream(j), 0, 0)),
                  pl.BlockSpec((D, tn), lambda bi, i, j: (0, j))],
        out_specs=pl.BlockSpec((None, tm, tn), lambda bi, i, j: (bi, i, j)),
        out_shape=jax.ShapeDtypeStruct((b, L, n), BF16),
        scratch_shapes=[pltpu.VMEM((tm, D), F32), pltpu.VMEM((tm, D), F32), pltpu.VMEM((tm, D), BF16)],
        compiler_params=_cparams(("parallel", "parallel", "arbitrary")),
        name="rwkv_proj",
    )(x, x, x, sc, sh, mu6, wcat)


def _seg_sum64(x, ones_blk):
    hi, lo = _split2(x)
    return _dot(hi, ones_blk) + _dot(lo, ones_blk)


def _rwkv_post_kernel(r_ref, k_ref, v_ref, hw_ref, ha_ref, hg_ref, w2_ref, a2_ref, g2_ref,
                      w0_ref, a0_ref, kk_ref, ka_ref, rk_ref,
                      ro_ref, kko_ref, vo_ref, bo_ref, go_ref, lw_ref, kd_ref, kao_ref):
    r = r_ref[...].astype(F32)
    k = k_ref[...].astype(F32)
    v = v_ref[...].astype(F32)
    tm, D = r.shape
    li = lax.broadcasted_iota(jnp.int32, (LANES, LANES), 0) // RWKV_HEADSIZE
    lj = lax.broadcasted_iota(jnp.int32, (LANES, LANES), 1) // RWKV_HEADSIZE
    ones_blk = jnp.where(li == lj, 1.0, 0.0).astype(BF16)
    g = _dot(jax.nn.sigmoid(hg_ref[...].astype(F32)).astype(BF16), g2_ref[...])
    thw = jnp.tanh(hw_ref[...].astype(F32)).astype(BF16)
    ha = ha_ref[...]
    kkr = k * kk_ref[...]
    rkr = r * k * rk_ref[...]
    aas = []
    for e in range(2):
        dw = _dot(thw, w2_ref[e])
        wl = -_softplus(-(w0_ref[e] + dw)) - 0.5
        lw = -jnp.exp(wl)
        a = jax.nn.sigmoid(a0_ref[e] + _dot(ha, a2_ref[e]))
        aas.append(a)
        kd = k * (1.0 + (a - 1.0) * ka_ref[...])
        for p in range(RWKV_PAIRS):
            sl = slice(p * LANES, (p + 1) * LANES)
            lw_ref[e, p] = lw[:, sl]
            kd_ref[e, p] = kd[:, sl].astype(kd_ref.dtype)
    for p in range(RWKV_PAIRS):
        sl = slice(p * LANES, (p + 1) * LANES)
        kp = kkr[:, sl]
        nrm = jnp.maximum(jnp.sqrt(_seg_sum64(kp * kp, ones_blk)), 1e-12)
        kkp = kp / nrm
        kko_ref[p] = kkp.astype(kko_ref.dtype)
        for e in range(2):
            kao_ref[e, p] = (kkp * aas[e][:, sl]).astype(kao_ref.dtype)
        bo_ref[p] = (_seg_sum64(rkr[:, sl], ones_blk) * v[:, sl]).astype(bo_ref.dtype)
        ro_ref[p] = r[:, sl].astype(ro_ref.dtype)
        vo_ref[p] = v[:, sl].astype(vo_ref.dtype)
        go_ref[p] = g[:, sl].astype(go_ref.dtype)


def rwkv_post(proj, w2z, a2z, g2, w0, a0, k_k, k_a, r_k, tm):
    b, L, _ = proj.shape
    D = D_MODEL
    NP = RWKV_PAIRS
    col = lambda off, w: pl.BlockSpec((None, tm, w), lambda bi, i: (bi, i, off // w))
    full = lambda shape: pl.BlockSpec(shape, lambda bi, i: (0,) * len(shape))
    pair = lambda: pl.BlockSpec((None, NP, tm, LANES), lambda bi, i: (bi, 0, i, 0))
    pair2 = lambda: pl.BlockSpec((2, None, NP, tm, LANES), lambda bi, i: (0, bi, 0, i, 0))
    sh1 = lambda dt: jax.ShapeDtypeStruct((b, NP, L, LANES), dt)
    sh2 = lambda dt: jax.ShapeDtypeStruct((2, b, NP, L, LANES), dt)
    return pl.pallas_call(
        _rwkv_post_kernel,
        grid=(b, L // tm),
        in_specs=[col(0, D), col(D, D), col(2 * D, D),
                  col(3 * D, LORA_PAD), col(3 * D + LORA_PAD, LORA_PAD), col(3 * D + 2 * LORA_PAD, LORA_PAD),
                  full((2, LORA_PAD, D)), full((2, LORA_PAD, D)), full((LORA_PAD, D)),
                  full((2, 1, D)), full((2, 1, D)), full((1, D)), full((1, D)), full((1, D))],
        out_specs=[pair(), pair(), pair(), pair(), pair(), pair2(), pair2(), pair2()],
        out_shape=[sh1(BF16), sh1(BF16), sh1(BF16), sh1(BF16), sh1(BF16), sh2(F32), sh2(BF16), sh2(BF16)],
        compiler_params=_cparams(("parallel", "parallel")),
        name="rwkv_post",
    )(proj, proj, proj, proj, proj, proj, w2z, a2z, g2, w0, a0, k_k, k_a, r_k)


def _wkv_kernel(r_ref, kk_ref, v_ref, lw_ref, kd_ref, ka_ref, s0_ref, y_ref, sfin_ref, st_ref):
    T = WKV_T
    d = pl.program_id(0)
    c = pl.program_id(2)
    nc = pl.num_programs(2)
    fwd = d == 0

    @pl.when(c == 0)
    def _():
        st_ref[...] = s0_ref[...]

    row = lax.broadcasted_iota(jnp.int32, (T, T), 0)
    col = lax.broadcasted_iota(jnp.int32, (T, T), 1)
    sgn = 1 - 2 * d
    tri = jnp.where((row - col) * sgn >= 0, 1.0, 0.0).astype(BF16)
    row2 = lax.broadcasted_iota(jnp.int32, (T, 2 * T), 0)
    col2 = lax.broadcasted_iota(jnp.int32, (T, 2 * T), 1) % T
    incl2 = (row2 - col2) * sgn >= 0
    strict2 = (row2 - col2) * sgn > 0
    lane = lax.broadcasted_iota(jnp.int32, (T, LANES), 1)
    m_a = lane < RWKV_HEADSIZE
    bi = lax.broadcasted_iota(jnp.int32, (LANES, LANES), 0) // RWKV_HEADSIZE
    bj = lax.broadcasted_iota(jnp.int32, (LANES, LANES), 1) // RWKV_HEADSIZE
    blockdiag = bi == bj
    eye2 = jnp.where(row2 == col2, 1.0, 0.0)
    blk = []
    size = 8
    while size <= T:
        blk.append(row2 // size == col2 // size)
        size *= 2

    def stack2(z):
        zero = jnp.zeros_like(z)
        return jnp.concatenate([jnp.where(m_a, z, zero), jnp.where(m_a, zero, z)], axis=0)

    def cast(z):
        return z.astype(BF16)

    def pairmul(xb, yb):
        return _dot(xb, stack2(yb))

    def each(f, *ls):
        return [f(*a) for a in zip(*ls)]

    ps = list(range(RWKV_PAIRS))
    lw = [lw_ref[p] for p in ps]
    hi = each(cast, lw)
    mid = each(lambda a, h_: cast(a - h_.astype(F32)), lw, hi)
    cum = each(lambda h_, m_: _dot(tri, h_) + _dot(tri, m_), hi, mid)
    e_neg = each(lambda cm: jnp.exp(-cm), cum)
    at = each(lambda p, cm, l_: -kk_ref[p].astype(F32) * jnp.exp(cm - l_), ps, cum, lw)
    rt = each(lambda p, cm: r_ref[p].astype(F32) * jnp.exp(cm), ps, cum)
    bt = each(lambda p, en: cast(ka_ref[p].astype(F32) * en), ps, e_neg)
    kt = each(lambda p, en: cast(kd_ref[p].astype(F32) * en), ps, e_neg)
    x1 = each(lambda a, r_: cast(jnp.concatenate([a, r_], axis=0)), at, rt)
    x2s = each(lambda b_, k_: jnp.concatenate([stack2(b_), stack2(k_)], axis=0), bt, kt)
    gall = each(_dot_nt, x1, x2s)
    lc = each(lambda g_: jnp.where(strict2, g_[:T, :2 * T], 0.0), gall)
    n1 = each(lambda l_: cast(jnp.where(blk[0], l_, 0.0)), lc)
    n2 = each(lambda a: cast(pairmul(a, a)), n1)
    n4 = each(lambda a: cast(pairmul(a, a)), n2)
    inv = each(lambda a: eye2 + a.astype(F32), n1)
    inv = each(lambda iv, a: iv + pairmul(cast(iv), a), inv, n2)
    inv = each(lambda iv, a: iv + pairmul(cast(iv), a), inv, n4)
    for lvl in range(1, len(blk)):
        offm = blk[lvl] & jnp.logical_not(blk[lvl - 1])
        ivb = each(cast, inv)
        t1 = each(lambda l_, ib: cast(pairmul(cast(jnp.where(offm, l_, 0.0)), ib)), lc, ivb)
        inv = each(lambda iv, ib, t_: iv + pairmul(ib, t_), inv, ivb, t1)
    invb = each(cast, inv)
    lak = each(lambda g_: cast(jnp.where(strict2, g_[:T, 2 * T:], 0.0)), gall)
    rbk = each(lambda g_: cast(jnp.concatenate([jnp.where(incl2, g_[T:, :2 * T], 0.0),
                                                jnp.where(incl2, g_[T:, 2 * T:], 0.0)], axis=1)), gall)
    v = [v_ref[p] for p in ps]
    v2 = each(stack2, v)
    s = [st_ref[p] for p in ps]
    h = each(lambda x_, s_: _dot_nt(x_, cast(s_)), x1, s)
    wv = each(lambda h_, l_, v_: h_[:T] + _dot(l_, v_), h, lak, v2)
    ub = each(lambda ib, w_: cast(pairmul(ib, cast(w_))), invb, wv)
    y = each(lambda h_, rb_, u_, v_: h_[T:] + _dot(rb_, jnp.concatenate([stack2(u_), v_], axis=0)), h, rbk, ub, v2)
    ds = each(lambda u_, v_, b_, k_: _dot_tn(jnp.concatenate([u_, v_], axis=0), jnp.concatenate([b_, k_], axis=0)),
              ub, v, bt, kt)
    for p in ps:
        y_ref[p] = y[p].astype(y_ref.dtype)
        c_end = jnp.where(fwd, cum[p][T - 1:T], cum[p][0:1])
        st_ref[p] = jnp.where(blockdiag, s[p] + ds[p], 0.0) * jnp.exp(c_end)

    @pl.when(c == nc - 1)
    def _():
        sfin_ref[...] = st_ref[...]


def wkv_scan(r, kk, v, lw, kd, ka, s0):
    b, NP, L, _ = r.shape
    T = WKV_T
    nc = L // T

    def cidx(d, c):
        return jnp.where(d == 0, c, nc - 1 - c)

    shared = lambda: pl.BlockSpec((None, NP, T, LANES), lambda d, bi, c: (bi, 0, cidx(d, c), 0))
    perdir = lambda: pl.BlockSpec((None, None, NP, T, LANES), lambda d, bi, c: (d, bi, 0, cidx(d, c), 0))
    state = lambda: pl.BlockSpec((None, None, NP, LANES, LANES), lambda d, bi, c: (d, bi, 0, 0, 0))
    return pl.pallas_call(
        _wkv_kernel,
        grid=(2, b, nc),
        in_specs=[shared(), shared(), shared(), perdir(), perdir(), perdir(), state()],
        out_specs=[perdir(), state()],
        out_shape=[jax.ShapeDtypeStruct((2, b, NP, L, LANES), F32),
                   jax.ShapeDtypeStruct((2, b, NP, LANES, LANES), F32)],
        scratch_shapes=[pltpu.VMEM((NP, LANES, LANES), F32)],
        compiler_params=_cparams(("parallel", "parallel", "arbitrary")),
        name="wkv_scan",
    )(r, kk, v, lw, kd, ka, s0)


def _odd_out_kernel(y0_ref, y1_ref, bo_ref, g_ref, lw_ref, lb_ref, w_ref, xres_ref, gate_ref,
                    lng_ref, lnb_ref, sc_ref, sh_ref, xo_ref, h2_ref, lhs_ref):
    li = lax.broadcasted_iota(jnp.int32, (LANES, LANES), 0) // RWKV_HEADSIZE
    lj = lax.broadcasted_iota(jnp.int32, (LANES, LANES), 1) // RWKV_HEADSIZE
    ones_blk = jnp.where(li == lj, 1.0, 0.0).astype(BF16)
    inv = 1.0 / RWKV_HEADSIZE
    for p in range(RWKV_PAIRS):
        sl = slice(p * LANES, (p + 1) * LANES)
        y = y0_ref[p] + y1_ref[p]
        mu = _seg_sum64(y, ones_blk) * inv
        dlt = y - mu
        var = _seg_sum64(dlt * dlt, ones_blk) * inv
        yn = dlt * lax.rsqrt(var + GN_EPS) * lw_ref[:, sl] + lb_ref[:, sl]
        lhs_ref[:, sl] = ((yn + bo_ref[p].astype(F32)) * g_ref[p].astype(F32)).astype(BF16)
    ymix = _dot(lhs_ref[...], w_ref[...])
    xn = _layer_norm(ALPHA * xres_ref[...] + gate_ref[...] * ymix, lng_ref[...], lnb_ref[...])
    xo_ref[...] = xn
    h2_ref[...] = xn * (1.0 + sc_ref[...]) + sh_ref[...]


def odd_out(y2, bonus, g, lnx_w, lnx_b, w_o, x_res, gate, ln_g, ln_b, sc2, sh2, tm):
    b, L, D = x_res.shape
    NP = RWKV_PAIRS
    vec = lambda: pl.BlockSpec((1, D), lambda bi, i: (0, 0))
    bvec = lambda: pl.BlockSpec((None, 1, D), lambda bi, i: (bi, 0, 0))
    pair = lambda: pl.BlockSpec((None, NP, tm, LANES), lambda bi, i: (bi, 0, i, 0))
    return pl.pallas_call(
        _odd_out_kernel,
        grid=(b, L // tm),
        in_specs=[pl.BlockSpec((None, None, NP, tm, LANES), lambda bi, i: (0, bi, 0, i, 0)),
                  pl.BlockSpec((None, None, NP, tm, LANES), lambda bi, i: (1, bi, 0, i, 0)),
                  pair(), pair(), vec(), vec(),
                  pl.BlockSpec((D, D), lambda bi, i: (0, 0)),
                  pl.BlockSpec((None, tm, D), lambda bi, i: (bi, i, 0)),
                  bvec(), vec(), vec(), bvec(), bvec()],
        out_specs=[pl.BlockSpec((None, tm, D), lambda bi, i: (bi, i, 0)),
                   pl.BlockSpec((None, tm, D), lambda bi, i: (bi, i, 0))],
        out_shape=[jax.ShapeDtypeStruct((b, L, D), F32), jax.ShapeDtypeStruct((b, L, D), F32)],
        scratch_shapes=[pltpu.VMEM((tm, D), BF16)],
        compiler_params=_cparams(("parallel", "parallel")),
        name="odd_out",
    )(y2, y2, bonus, g, lnx_w, lnx_b, w_o, x_res, gate, ln_g, ln_b, sc2, sh2)


def _pad_cols(w, n):
    return jnp.pad(w, ((0, 0), (0, n - w.shape[1])))


def even_layer(x_lat, x_ctx, ml, mc, p):
    bsz = x_lat.shape[0]
    o1, o2, o3 = SSD_WIDTH, SSD_WIDTH + SSD_CONV_DIM, SSD_WIDTH + SSD_CONV_DIM + 2 * SSD_HEADS
    w_in = p["w_in"]
    w_pad = jnp.concatenate([w_in[:, :o1], w_in[:, o1:o2], w_in[:, o3:],
                             _pad_cols(w_in[:, o2:o3], PR_N - PR_DT)], axis=1).astype(BF16)
    dt_bias128 = _pad_cols(p["dt_bias"].reshape(1, 2 * SSD_HEADS), LANES)
    a_log128 = _pad_cols(p["a_log"].reshape(1, 2 * SSD_HEADS), LANES)
    dskip = jnp.repeat(p["d_skip"], SSD_HEADDIM).reshape(1, SSD_WIDTH)
    norm_w = p["norm_w"].reshape(1, SSD_WIDTH)
    conv_b = p["conv_b"].reshape(1, SSD_CONV_DIM)
    w_out = p["w_out"].astype(BF16)
    cc, sc = _dft_tables(FNET_GW)
    cs = jnp.asarray(np.concatenate([cc, sc], axis=1), BF16)

    def run(x, m, s0):
        L = x.shape[1]
        tm = min(L, 512)
        pr = modmm(x, m[:, 1], m[:, 0], w_pad, tm, 768)
        xbc = conv_silu(pr, p["conv_w"], conv_b)
        y2, s_fin = ssd_scan(xbc, pr, dt_bias128, a_log128, s0)
        cl, sl = _dft_tables(L)
        wpos = jnp.asarray(np.concatenate([cl, -sl], axis=1), BF16)
        xcs = fnet_chan(pr, cs, tm).reshape(bsz, 2 * L, FNET_WIDTH)
        fmix = fnet_pos(wpos, xcs, min(L, 512), 512)
        x_new, h2 = even_out(y2, xbc, pr, fmix, dskip, norm_w, w_out, x, m[:, 2], p["ln_g"], p["ln_b"],
                             m[:, 4], m[:, 3], min(L, 256))
        return x_new, h2, s_fin

    s0 = jnp.zeros((2, bsz, SSD_GROUPS, SSD_STATE, SSD_HPG * SSD_HEADDIM), F32)
    xc_new, h2c, s_ctx = run(x_ctx, mc, s0)
    xl_new, h2l, _ = run(x_lat, ml, s_ctx)
    return xl_new, h2l, xc_new, h2c


def odd_layer(x_lat, x_ctx, ml, mc, p, need_ctx):
    bsz = x_lat.shape[0]
    D = D_MODEL
    mu = p["mu"]
    mu6 = jnp.stack([mu[0], mu[2], mu[3], mu[1], mu[4], mu[5]]).reshape(6, 1, D)
    w_rkv = p["w_rkv"]
    w1cat = _pad_cols(jnp.concatenate([p["w1"][0], p["w1"][1]], axis=1), LORA_PAD)
    a1cat = _pad_cols(jnp.concatenate([p["a1"][0], p["a1"][1]], axis=1), LORA_PAD)
    wcat = jnp.concatenate([w_rkv[0], w_rkv[1], w_rkv[2], w1cat, a1cat, p["g1"]], axis=1).astype(BF16)

    def lora2(w2):
        z = jnp.zeros((2, LORA_PAD, D), F32)
        z = z.at[0, 0:w2.shape[1]].set(w2[0])
        z = z.at[1, w2.shape[1]:2 * w2.shape[1]].set(w2[1])
        return z.astype(BF16)

    w2z = lora2(p["w2"])
    a2z = lora2(p["a2"])
    g2 = p["g2"].astype(BF16)
    w0 = p["w0"].reshape(2, 1, D)
    a0 = p["a0"].reshape(2, 1, D)
    k_k = p["k_k"].reshape(1, D)
    k_a = p["k_a"].reshape(1, D)
    r_k = p["r_k"].reshape(1, D)
    lnx_w = p["lnx_w"].reshape(1, D)
    lnx_b = p["lnx_b"].reshape(1, D)
    w_o = p["w_o"].astype(BF16)

    def run(x, m, s0, grid_w, need_out):
        L = x.shape[1]
        tm = min(L, 512)
        proj = rwkv_proj(x, m[:, 1], m[:, 0], mu6, wcat, tm, 256, grid_w)
        r, kk, v, bonus, g, lw, kd, ka = rwkv_post(proj, w2z, a2z, g2, w0, a0, k_k, k_a, r_k, min(L, 256))
        y2, s_fin = wkv_scan(r, kk, v, lw, kd, ka, s0)
        if not need_out:
            return None, None, s_fin
        x_new, h2 = odd_out(y2, bonus, g, lnx_w, lnx_b, w_o, x, m[:, 2], p["ln_g"], p["ln_b"],
                            m[:, 4], m[:, 3], min(L, 256))
        return x_new, h2, s_fin

    s0 = jnp.zeros((2, bsz, RWKV_PAIRS, LANES, LANES), F32)
    xc_new, h2c, s_ctx = run(x_ctx, mc, s0, None, need_ctx)
    xl_new, h2l, _ = run(x_lat, ml, s_ctx, 64, True)
    return xl_new, h2l, xc_new, h2c


def kernel(x, c, ctx, c_ctx, w_mod, b_mod, ln_g, ln_b, ssd_w_in, ssd_conv_w, ssd_conv_b, ssd_a_log, ssd_dt_bias, ssd_d, ssd_norm_w, even_w_out, rwkv_mu, rwkv_w_rkv, rwkv_w_o, rwkv_w0, rwkv_w1, rwkv_w2, rwkv_a0, rwkv_a1, rwkv_a2, rwkv_g1, rwkv_g2, rwkv_k_k, rwkv_k_a, rwkv_r_k, rwkv_lnx_w, rwkv_lnx_b, router_w, router_bias, moe_w_gate, moe_w_up, moe_w_down):
    bsz, L, D = x.shape
    Lc = ctx.shape[1]
    n_lat = bsz * L
    router_w128 = _pad_cols(router_w, LANES)
    router_b128 = jnp.pad(router_bias.reshape(1, N_EXPERTS), ((0, 0), (0, LANES - N_EXPERTS)),
                          constant_values=-jnp.inf)
    cc = jnp.concatenate([c, c_ctx[None, :], jnp.zeros((7, D), F32)], axis=0)
    x_lat, x_ctx = x, ctx
    for i in range(DEPTH):
        last = i == DEPTH - 1
        j = i // 2
        m_all = dense_silu(cc, w_mod, i, b_mod[i].reshape(1, 6 * D))
        ml = m_all[:bsz].reshape(bsz, 6, 1, D)
        mc = jnp.broadcast_to(m_all[bsz].reshape(1, 6, 1, D), (bsz, 6, 1, D))
        lng0, lnb0 = ln_g[i, 0].reshape(1, D), ln_b[i, 0].reshape(1, D)
        lng1, lnb1 = ln_g[i, 1].reshape(1, D), ln_b[i, 1].reshape(1, D)
        if i % 2 == 0:
            p = dict(w_in=ssd_w_in[j], conv_w=ssd_conv_w[j], conv_b=ssd_conv_b[j], a_log=ssd_a_log[j],
                     dt_bias=ssd_dt_bias[j], d_skip=ssd_d[j], norm_w=ssd_norm_w[j], w_out=even_w_out[j],
                     ln_g=lng0, ln_b=lnb0)
            xl, h2l, xc, h2c = even_layer(x_lat, x_ctx, ml, mc, p)
        else:
            p = dict(mu=rwkv_mu[j], w_rkv=rwkv_w_rkv[j], w_o=rwkv_w_o[j], w0=rwkv_w0[j], w1=rwkv_w1[j],
                     w2=rwkv_w2[j], a0=rwkv_a0[j], a1=rwkv_a1[j], a2=rwkv_a2[j], g1=rwkv_g1[j],
                     g2=rwkv_g2[j], k_k=rwkv_k_k[j], k_a=rwkv_k_a[j], r_k=rwkv_r_k[j],
                     lnx_w=rwkv_lnx_w[j], lnx_b=rwkv_lnx_b[j], ln_g=lng0, ln_b=lnb0)
            xl, h2l, xc, h2c = odd_layer(x_lat, x_ctx, ml, mc, p, not last)
        tm = 256
        out = moe_layer(h2l.reshape(n_lat, D), xl.reshape(n_lat, D), ml[:, 5],
                        lambda t: t // (L // tm), router_w128, router_b128,
                        moe_w_gate, moe_w_up, moe_w_down, i, lng1, lnb1, tm)
        x_lat = out.reshape(bsz, L, D)
        if not last:
            out_c = moe_layer(h2c.reshape(bsz * Lc, D), xc.reshape(bsz * Lc, D), mc[:, 5],
                              lambda t: t // (Lc // tm), router_w128, router_b128,
                              moe_w_gate, moe_w_up, moe_w_down, i, lng1, lnb1, tm)
            x_ctx = out_c.reshape(bsz, Lc, D)
    return x_lat
```
